```python
import jax, jax.numpy as jnp
from jax import lax
import numpy as np

D_MODEL = 2048
BATCH = 16
SEQ = 2048
DEPTH = 2

CTX_LEN = 256
GRID_W = 64
EPS = 1e-6

N_HEADS = D_MODEL // 256
Q_LORA = D_MODEL // 4
KV_LORA = D_MODEL // 8
QK_NOPE = 128
QK_ROPE = 64
QK_HEAD = QK_NOPE + QK_ROPE
V_HEAD = 128
ATTN_W = N_HEADS * V_HEAD
AX_FREQS = QK_ROPE // 4
ROPE_THETA = 10000.0
ATTN_SCALE = QK_HEAD ** -0.5
Q_BLOCK = 128

CONV_W = D_MODEL // 4
CONV_K = 3

FOURIER_W = D_MODEL // 4
FOURIER_GROUPS = 4
FOURIER_GROUP_W = FOURIER_W // FOURIER_GROUPS

OFF_CQ = 0
OFF_CKV = OFF_CQ + Q_LORA
OFF_KR = OFF_CKV + KV_LORA
OFF_CB = OFF_KR + QK_ROPE
OFF_CC = OFF_CB + CONV_W
OFF_CX = OFF_CC + CONV_W
OFF_F = OFF_CX + CONV_W
IN_COLS = OFF_F + FOURIER_W
MIX_W = ATTN_W + CONV_W + FOURIER_W

N_GROUPS = 8
EXPERTS_PER_GROUP = 8
N_EXPERTS = N_GROUPS * EXPERTS_PER_GROUP
TOP_K = 2
D_EXPERT = D_MODEL // 4
MOE_BLOCK = 256

kernel_name = 'hymba_mla_conv_fnet_hmoe_dit'


def rms_norm(t, w):
    tf = t.astype(jnp.float32)
    y = tf * lax.rsqrt(jnp.mean(tf * tf, axis=-1, keepdims=True) + EPS)
    return (y * w.astype(jnp.float32)).astype(t.dtype)


def modulate(t, shift, scale):
    return t * (1.0 + scale) + shift


def axial_rope_tables(n_tok):
    n_rows = n_tok // GRID_W
    row = jnp.repeat(jnp.arange(n_rows, dtype=jnp.float32), GRID_W)
    col = (jnp.arange(n_rows * GRID_W) % GRID_W).astype(jnp.float32)
    inv = ROPE_THETA ** (-jnp.arange(AX_FREQS, dtype=jnp.float32) / AX_FREQS)
    ang_r = row[:, None, None] * inv
    ang_c = col[:, None, None] * inv
    return (jnp.cos(ang_r), jnp.sin(ang_r), jnp.cos(ang_c), jnp.sin(ang_c))


def rope_rotate(t, cos, sin):
    t1, t2 = jnp.split(t, 2, axis=-1)
    return jnp.concatenate([t1 * cos - t2 * sin, t2 * cos + t1 * sin], axis=-1)


def apply_axial_rope(t, tabs):
    cr, sr, cc, sc = tabs
    rope = t[..., QK_NOPE:].astype(jnp.float32)
    half = QK_ROPE // 2
    rot = jnp.concatenate([rope_rotate(rope[..., :half], cr, sr),
                           rope_rotate(rope[..., half:], cc, sc)], axis=-1)
    return jnp.concatenate([t[..., :QK_NOPE], rot.astype(t.dtype)], axis=-1)


def mla_queries(c_q, q_a_norm_w, w_uq, q_norm_w):
    b, n, _ = c_q.shape
    q = (rms_norm(c_q, q_a_norm_w) @ w_uq).reshape(b, n, N_HEADS, QK_HEAD)
    return rms_norm(q, q_norm_w)


def mla_keys_values(c_kv, k_rope, kv_a_norm_w, w_ukv, k_norm_w):
    b, n, _ = c_kv.shape
    kv = (rms_norm(c_kv, kv_a_norm_w) @ w_ukv).reshape(b, n, N_HEADS, QK_NOPE + V_HEAD)
    k_nope, v = kv[..., :QK_NOPE], kv[..., QK_NOPE:]
    k_r = jnp.broadcast_to(k_rope[:, :, None, :], (b, n, N_HEADS, QK_ROPE))
    k = jnp.concatenate([k_nope, k_r], axis=-1)
    return rms_norm(k, k_norm_w), v


def block_attention(q, k, v):
    b, n, h, _ = q.shape
    nb = n // Q_BLOCK
    qb = q.reshape(b, nb, Q_BLOCK, h, QK_HEAD).transpose(1, 0, 2, 3, 4)

    def one_block(q_blk):
        s = jnp.einsum('bqhd,bkhd->bhqk', q_blk, k).astype(jnp.float32) * ATTN_SCALE
        p = jax.nn.softmax(s, axis=-1).astype(v.dtype)
        return jnp.einsum('bhqk,bkhd->bqhd', p, v)

    o = lax.map(one_block, qb)
    return o.transpose(1, 0, 2, 3, 4).reshape(b, n, h * V_HEAD)


def short_conv(u, w):
    up = jnp.pad(u, ((0, 0), (1, 1), (0, 0)))
    return up[:, :-2] * w[0] + up[:, 1:-1] * w[1] + up[:, 2:] * w[2]


def conv_mixer(p, conv_w):
    gate_b = p[..., OFF_CB:OFF_CC]
    gate_c = p[..., OFF_CC:OFF_CX]
    u = p[..., OFF_CX:OFF_F]
    return gate_b * short_conv(gate_c * u, conv_w)


def fourier_mix(u):
    b, n, _ = u.shape
    uf = u.astype(jnp.float32).reshape(b, n, FOURIER_GROUPS, FOURIER_GROUP_W)
    y = jnp.fft.fftn(uf, axes=(1, 3), norm='ortho').real
    return y.reshape(b, n, FOURIER_W).astype(u.dtype)


def hier_moe(h, w_rg, b_rg, w_re, b_re, w_gate, w_up, w_down):
    n_tok, d = h.shape
    hf = h.astype(jnp.float32)
    g_logits = hf @ w_rg.astype(jnp.float32) + b_rg.astype(jnp.float32)
    g_prob = jax.nn.softmax(g_logits, axis=-1)
    g_idx = jnp.argmax(g_logits, axis=-1)
    g_p = jnp.take_along_axis(g_prob, g_idx[:, None], axis=1)
    e_logits = (hf @ w_re.astype(jnp.float32) + b_re.astype(jnp.float32)).reshape(n_tok, N_GROUPS, EXPERTS_PER_GROUP)
    e_logits = jnp.take_along_axis(e_logits, g_idx[:, None, None], axis=1)[:, 0]
    top_p, top_i = lax.top_k(jax.nn.softmax(e_logits, axis=-1), TOP_K)
    gates = g_p * top_p / jnp.sum(top_p, axis=-1, keepdims=True)
    expert = (g_idx[:, None] * EXPERTS_PER_GROUP + top_i).astype(jnp.int32)

    n_asg = n_tok * TOP_K
    e_flat = expert.reshape(-1)
    tok_flat = jnp.arange(n_asg, dtype=jnp.int32) // TOP_K
    order = jnp.argsort(e_flat)
    e_s = e_flat[order]
    tok_s = tok_flat[order]
    w_s = gates.reshape(-1)[order]
    counts = jnp.bincount(e_flat, length=N_EXPERTS).astype(jnp.int32)
    starts = jnp.cumsum(counts) - counts
    padded = (counts + MOE_BLOCK - 1) // MOE_BLOCK * MOE_BLOCK
    pad_ends = jnp.cumsum(padded)
    pad_starts = pad_ends - padded
    slot = pad_starts[e_s] + (jnp.arange(n_asg, dtype=jnp.int32) - starts[e_s])
    n_blocks = -(-n_asg // MOE_BLOCK) + N_EXPERTS
    n_slots = n_blocks * MOE_BLOCK
    slot_tok = jnp.full((n_slots,), n_tok, jnp.int32).at[slot].set(tok_s)
    slot_w = jnp.zeros((n_slots,), jnp.float32).at[slot].set(w_s)
    block_start = jnp.arange(n_blocks, dtype=jnp.int32) * MOE_BLOCK
    block_exp = jnp.minimum(jnp.searchsorted(pad_ends, block_start, side='right'), N_EXPERTS - 1)
    h_pad = jnp.concatenate([h, jnp.zeros((1, d), h.dtype)], axis=0)

    def expert_block(args):
        tok_b, w_b, e = args
        xb = h_pad[tok_b]
        hid = jax.nn.silu(xb @ w_gate[e]) * (xb @ w_up[e])
        return (hid @ w_down[e]) * w_b[:, None].astype(h.dtype)

    y = lax.map(expert_block, (slot_tok.reshape(n_blocks, MOE_BLOCK),
                               slot_w.reshape(n_blocks, MOE_BLOCK), block_exp))
    out = jax.ops.segment_sum(y.reshape(n_slots, d), slot_tok, num_segments=n_tok + 1)
    return out[:n_tok]


def hybrid_layer(x, xc, mod, mod_c, norm1_w, norm2_w, w_in, q_a_norm_w, w_uq, kv_a_norm_w, w_ukv,
                 q_norm_w, k_norm_w, conv_w, w_out, w_rg, b_rg, w_re, b_re, w_gate, w_up, w_down,
                 tabs, last):
    b, s, d = x.shape
    sh1, sc1, g1, sh2, sc2, g2 = jnp.split(mod[:, None, :], 6, axis=-1)
    csh1, csc1, cg1, csh2, csc2, cg2 = jnp.split(mod_c, 6)

    h = modulate(rms_norm(x, norm1_w), sh1, sc1)
    hc = modulate(rms_norm(xc, norm1_w), csh1, csc1)
    p = h @ w_in
    if last:
        pc_kv = hc @ w_in[:, OFF_CKV:OFF_CB]
        c_kv_c, kr_c = pc_kv[..., :KV_LORA], pc_kv[..., KV_LORA:]
    else:
        pc = hc @ w_in
        c_kv_c, kr_c = pc[..., OFF_CKV:OFF_KR], pc[..., OFF_KR:OFF_CB]
    kc, vc = mla_keys_values(c_kv_c, kr_c, kv_a_norm_w, w_ukv, k_norm_w)

    q = apply_axial_rope(mla_queries(p[..., OFF_CQ:OFF_CKV], q_a_norm_w, w_uq, q_norm_w), tabs)
    k, v = mla_keys_values(p[..., OFF_CKV:OFF_KR], p[..., OFF_KR:OFF_CB], kv_a_norm_w, w_ukv, k_norm_w)
    k = apply_axial_rope(k, tabs)
    attn = block_attention(q, jnp.concatenate([kc, k], axis=1), jnp.concatenate([vc, v], axis=1))
    mix = jnp.concatenate([attn, conv_mixer(p, conv_w), fourier_mix(p[..., OFF_F:])], axis=-1) @ w_out
    x = x + g1 * mix

    if not last:
        qc = mla_queries(pc[..., OFF_CQ:OFF_CKV], q_a_norm_w, w_uq, q_norm_w)
        attn_c = block_attention(qc, kc, vc)
        mix_c = jnp.concatenate([attn_c, conv_mixer(pc, conv_w), fourier_mix(pc[..., OFF_F:])], axis=-1) @ w_out
        xc = xc + cg1 * mix_c

    h2 = modulate(rms_norm(x, norm2_w), sh2, sc2)
    if last:
        ffn = hier_moe(h2.reshape(-1, d), w_rg, b_rg, w_re, b_re, w_gate, w_up, w_down).reshape(b, s, d)
        return x + g2 * ffn, xc
    h2c = modulate(rms_norm(xc, norm2_w), csh2, csc2)
    n_ctx = xc.shape[1]
    toks = jnp.concatenate([h2.reshape(-1, d), h2c.reshape(-1, d)], axis=0)
    ffn = hier_moe(toks, w_rg, b_rg, w_re, b_re, w_gate, w_up, w_down)
    x = x + g2 * ffn[:b * s].reshape(b, s, d)
    xc = xc + cg2 * ffn[b * s:].reshape(b, n_ctx, d)
    return x, xc


def setup_inputs(seed: int = 0) -> dict:
    key = jax.random.key(seed)
    ks = jax.random.split(key, 24)
    L, D = DEPTH, D_MODEL

    def nrm(k, shape, scale):
        return jax.random.normal(k, shape, jnp.float32) * scale

    return {
        'x': nrm(ks[0], (BATCH, SEQ, D), 1.0),
        'c': nrm(ks[1], (BATCH, D), 1.0),
        'ctx': nrm(ks[2], (BATCH, CTX_LEN, D), 1.0),
        'c_ctx': nrm(ks[3], (D,), 1.0),
        'w_mod': nrm(ks[4], (L, D, 6 * D), D ** -0.5),
        'b_mod': nrm(ks[5], (L, 6 * D), 0.01),
        'norm1_w': 1.0 + nrm(ks[6], (L, D), 0.01),
        'norm2_w': 1.0 + nrm(ks[7], (L, D), 0.01),
        'w_in': nrm(ks[8], (L, D, IN_COLS), D ** -0.5),
        'q_a_norm_w': 1.0 + nrm(ks[9], (L, Q_LORA), 0.01),
        'w_uq': nrm(ks[10], (L, Q_LORA, N_HEADS * QK_HEAD), Q_LORA ** -0.5),
        'kv_a_norm_w': 1.0 + nrm(ks[11], (L, KV_LORA), 0.01),
        'w_ukv': nrm(ks[12], (L, KV_LORA, N_HEADS * (QK_NOPE + V_HEAD)), KV_LORA ** -0.5),
        'q_norm_w': 1.0 + nrm(ks[13], (L, QK_HEAD), 0.01),
        'k_norm_w': 1.0 + nrm(ks[14], (L, QK_HEAD), 0.01),
        'conv_w': nrm(ks[15], (L, CONV_K, CONV_W), CONV_K ** -0.5),
        'w_out': nrm(ks[16], (L, MIX_W, D), MIX_W ** -0.5),
        'w_router_group': nrm(ks[17], (L, D, N_GROUPS), D ** -0.5),
        'b_router_group': nrm(ks[18], (L, N_GROUPS), 0.01),
        'w_router_expert': nrm(ks[19], (L, D, N_EXPERTS), D ** -0.5),
        'b_router_expert': nrm(ks[20], (L, N_EXPERTS), 0.01),
        'w_gate': nrm(ks[21], (L, N_EXPERTS, D, D_EXPERT), D ** -0.5),
        'w_up': nrm(ks[22], (L, N_EXPERTS, D, D_EXPERT), D ** -0.5),
        'w_down': nrm(ks[23], (L, N_EXPERTS, D_EXPERT, D), D_EXPERT ** -0.5),
    }


def reference(x, c, ctx, c_ctx, w_mod, b_mod, norm1_w, norm2_w, w_in, q_a_norm_w, w_uq, kv_a_norm_w,
              w_ukv, q_norm_w, k_norm_w, conv_w, w_out, w_router_group, b_router_group,
              w_router_expert, b_router_expert, w_gate, w_up, w_down):
    tabs = axial_rope_tables(x.shape[1])
    sc = jax.nn.silu(c)
    sc_ctx = jax.nn.silu(c_ctx)
    xc = ctx
    for l in range(DEPTH):
        mod = sc @ w_mod[l] + b_mod[l]
        mod_c = sc_ctx @ w_mod[l] + b_mod[l]
        x, xc = hybrid_layer(x, xc, mod, mod_c, norm1_w[l], norm2_w[l], w_in[l], q_a_norm_w[l], w_uq[l],
                             kv_a_norm_w[l], w_ukv[l], q_norm_w[l], k_norm_w[l], conv_w[l], w_out[l],
                             w_router_group[l], b_router_group[l], w_router_expert[l], b_router_expert[l],
                             w_gate[l], w_up[l], w_down[l], tabs, l == DEPTH - 1)
    return x
```

```python
import functools
import math

import jax
import jax.numpy as jnp
from jax import lax
from jax.experimental import pallas as pl
from jax.experimental.pallas import tpu as pltpu

F32 = jnp.float32
BF16 = jnp.bfloat16

EPS = 1e-6
N_HEADS = 8
QK_NOPE = 128
QK_ROPE = 64
QK_HEAD = QK_NOPE + QK_ROPE
V_HEAD = 128
GRID_W = 64
ROPE_THETA = 10000.0
ATTN_SCALE = QK_HEAD ** -0.5
CONV_K = 3
FOURIER_GROUPS = 4
N_GROUPS = 8
EXPERTS_PER_GROUP = 8
N_EXPERTS = N_GROUPS * EXPERTS_PER_GROUP
TOP_K = 2
MOE_BLOCK = 256

LANES = 128
HEAD_SLOT = 2 * LANES
ROW_CHUNKS = 16
VMEM_LIMIT = 56 * 1024 * 1024
NEG_BIG = -1e30


def _cparams(n_axes):
    return pltpu.CompilerParams(dimension_semantics=("arbitrary",) * n_axes,
                                vmem_limit_bytes=VMEM_LIMIT)


def _const_spec(shape):
    nd = len(shape)
    return pl.BlockSpec(shape, lambda *_: (0,) * nd)


def _mod_kernel(c_ref, w_ref, b_ref, o_ref):
    c = c_ref[...]
    a = (c * jax.nn.sigmoid(c)).astype(BF16)
    o_ref[0] = jnp.dot(a, w_ref[0].astype(BF16), preferred_element_type=F32) + b_ref[0]


def _modulation(c_all, w_mod, b_mod):
    n_layers, d, n_out = w_mod.shape
    rows = c_all.shape[0]
    tn = 1024
    return pl.pallas_call(
        _mod_kernel,
        grid=(n_layers, n_out // tn),
        in_specs=[pl.BlockSpec((rows, d), lambda l, j: (0, 0)),
                  pl.BlockSpec((1, d, tn), lambda l, j: (l, 0, j)),
                  pl.BlockSpec((1, 1, tn), lambda l, j: (l, 0, j))],
        out_specs=pl.BlockSpec((1, rows, tn), lambda l, j: (l, 0, j)),
        out_shape=jax.ShapeDtypeStruct((n_layers, rows, n_out), F32),
        compiler_params=_cparams(2),
        name="modulation",
    )(c_all, w_mod, b_mod.reshape(n_layers, 1, n_out))


def _rms(t, w):
    return t * lax.rsqrt(jnp.mean(t * t, axis=-1, keepdims=True) + EPS) * w


def _rope_partner(t):
    lane = lax.broadcasted_iota(jnp.int32, t.shape, 1)
    ahead = pltpu.roll(t, LANES - 16, axis=1)
    behind = pltpu.roll(t, 16, axis=1)
    return jnp.where((lane & 16) == 0, ahead, behind)


def _inproj_kernel(x_ref, mod_ref, n1w_ref, win_ref, qanw_ref, wuq_ref, kvanw_ref, wukv_ref,
                   qnw_ref, knw_ref, rc_ref, rs_ref, wdft_ref, *out_refs, cols, kv_only):
    if kv_only:
        k_ref, v_ref = out_refs
    else:
        q_ref, k_ref, v_ref, gb_ref, g_ref, fab_ref = out_refs
    x = x_ref[0]
    shift, scale = mod_ref[0, 0:1, :], mod_ref[0, 1:2, :]
    hb = (_rms(x, n1w_ref[...]) * (1.0 + scale) + shift).astype(BF16)
    rope_c, rope_s = rc_ref[...], rs_ref[...]

    def rope(t):
        return t * rope_c + _rope_partner(t) * rope_s

    if kv_only:
        def col(name):
            lo, hi = cols[name]
            return jnp.dot(hb, win_ref[:, lo:hi], preferred_element_type=F32)
    else:
        p = jnp.dot(hb, win_ref[...], preferred_element_type=F32)

        def col(name):
            lo, hi = cols[name]
            return p[:, lo:hi]

    ckv = _rms(col("ckv"), kvanw_ref[...]).astype(BF16)
    kv = jnp.dot(ckv, wukv_ref[...], preferred_element_type=F32)
    k_r = col("kr")
    ss_r = jnp.sum(k_r * k_r, axis=-1, keepdims=True)
    knw0, knw1 = knw_ref[:, :LANES], knw_ref[:, LANES:]
    kr_rot = rope(k_r * knw1)
    v_off = N_HEADS * QK_NOPE
    for h in range(N_HEADS):
        kn = kv[:, h * QK_NOPE:(h + 1) * QK_NOPE]
        r = lax.rsqrt((jnp.sum(kn * kn, axis=-1, keepdims=True) + ss_r) * (1.0 / QK_HEAD) + EPS)
        k_ref[0, h, :, :LANES] = (kn * r * knw0).astype(BF16)
        k_ref[0, h, :, LANES:] = (kr_rot * r).astype(BF16)
        v_ref[0, h] = kv[:, v_off + h * V_HEAD:v_off + (h + 1) * V_HEAD].astype(BF16)
    if kv_only:
        return

    cq = _rms(col("cq"), qanw_ref[...]).astype(BF16)
    qraw = jnp.dot(cq, wuq_ref[...], preferred_element_type=F32)
    qnw0, qnw1 = qnw_ref[:, :LANES], qnw_ref[:, LANES:]
    for h in range(N_HEADS):
        q0 = qraw[:, h * HEAD_SLOT:h * HEAD_SLOT + LANES]
        q1 = qraw[:, h * HEAD_SLOT + LANES:(h + 1) * HEAD_SLOT]
        ss = jnp.sum(q0 * q0, axis=-1, keepdims=True) + jnp.sum(q1 * q1, axis=-1, keepdims=True)
        r = lax.rsqrt(ss * (1.0 / QK_HEAD) + EPS) * ATTN_SCALE
        q_ref[0, h, :, :LANES] = (q0 * r * qnw0).astype(BF16)
        q_ref[0, h, :, LANES:] = rope(q1 * r * qnw1).astype(BF16)

    gb_ref[0] = col("gb").astype(BF16)
    g_ref[0] = (col("gc") * col("u")).astype(BF16)
    f = col("f").astype(BF16)
    for g in range(FOURIER_GROUPS):
        res = jnp.dot(f[:, g * LANES:(g + 1) * LANES], wdft_ref[...], preferred_element_type=F32)
        fab_ref[0, 0, :, g * LANES:(g + 1) * LANES] = res[:, :LANES].astype(BF16)
        fab_ref[0, 1, :, g * LANES:(g + 1) * LANES] = res[:, LANES:].astype(BF16)


def _inproj(x, mods, lw, rope_c, rope_s, wdft, *, kv_only):
    nb, s, d = x.shape
    tm = min(s, 256)
    cols = lw["cols"]
    fw = wdft.shape[0] * FOURIER_GROUPS
    cw = cols["gb"][1] - cols["gb"][0]
    kernel = functools.partial(_inproj_kernel, cols=cols, kv_only=kv_only)
    head_spec = lambda w: pl.BlockSpec((1, N_HEADS, tm, w), lambda b, i: (b, 0, i, 0))
    tok_spec = lambda w: pl.BlockSpec((1, tm, w), lambda b, i: (b, i, 0))
    out_shape = [jax.ShapeDtypeStruct((nb, N_HEADS, s, HEAD_SLOT), BF16),
                 jax.ShapeDtypeStruct((nb, N_HEADS, s, V_HEAD), BF16)]
    out_specs = [head_spec(HEAD_SLOT), head_spec(V_HEAD)]
    if not kv_only:
        out_shape = ([jax.ShapeDtypeStruct((nb, N_HEADS, s, HEAD_SLOT), BF16)] + out_shape +
                     [jax.ShapeDtypeStruct((nb, s, cw), BF16), jax.ShapeDtypeStruct((nb, s, cw), BF16),
                      jax.ShapeDtypeStruct((nb, 2, s, fw), BF16)])
        out_specs = ([head_spec(HEAD_SLOT)] + out_specs +
                     [tok_spec(cw), tok_spec(cw),
                      pl.BlockSpec((1, 2, tm, fw), lambda b, i: (b, 0, i, 0))])
    weights = [lw["n1w"], lw["w_in"], lw["qanw"], lw["w_uq"], lw["kvanw"], lw["w_ukv"], lw["qnw"], lw["knw"]]
    in_specs = ([pl.BlockSpec((1, tm, d), lambda b, i: (b, i, 0)),
                 pl.BlockSpec((1, 6, d), lambda b, i: (b, 0, 0))] +
                [_const_spec(w.shape) for w in weights] +
                [pl.BlockSpec((tm, LANES), lambda b, i: (i, 0)),
                 pl.BlockSpec((tm, LANES), lambda b, i: (i, 0)),
                 _const_spec(wdft.shape)])
    return pl.pallas_call(
        kernel, grid=(nb, s // tm), in_specs=in_specs, out_specs=out_specs, out_shape=out_shape,
        compiler_params=_cparams(2), name="inproj_kv" if kv_only else "inproj",
    )(x, mods, lw["n1w"], lw["w_in"], lw["qanw"], lw["w_uq"], lw["kvanw"], lw["w_ukv"], lw["qnw"],
      lw["knw"], rope_c, rope_s, wdft)


_NT = (((1,), (1,)), ((), ()))


def _attn_kernel(q_ref, kc_ref, vc_ref, *rest, has_seq):
    o_ref = rest[-1]
    q = q_ref[0, 0]
    sc = lax.dot_general(q, kc_ref[0, 0], _NT, preferred_element_type=F32)
    m = jnp.max(sc, axis=-1, keepdims=True)
    if has_seq:
        k_ref, v_ref = rest[0], rest[1]
        ss = lax.dot_general(q, k_ref[0, 0], _NT, preferred_element_type=F32)
        m = jnp.maximum(m, jnp.max(ss, axis=-1, keepdims=True))
    pc = jnp.exp(sc - m)
    l = jnp.sum(pc, axis=-1, keepdims=True)
    o = jnp.dot(pc.astype(BF16), vc_ref[0, 0], preferred_element_type=F32)
    if has_seq:
        ps = jnp.exp(ss - m)
        l = l + jnp.sum(ps, axis=-1, keepdims=True)
        o = o + jnp.dot(ps.astype(BF16), v_ref[0, 0], preferred_element_type=F32)
    o_ref[0] = (o / l).astype(BF16)


def _attention(q, kc, vc, k=None, v=None):
    nb, nh, s, _ = q.shape
    n_ctx = kc.shape[2]
    has_seq = k is not None
    tq = min(s, 512)
    kv_spec = lambda n, w: pl.BlockSpec((1, 1, n, w), lambda b, h, i: (b, h, 0, 0))
    in_specs = [pl.BlockSpec((1, 1, tq, HEAD_SLOT), lambda b, h, i: (b, h, i, 0)),
                kv_spec(n_ctx, HEAD_SLOT), kv_spec(n_ctx, V_HEAD)]
    args = [q, kc, vc]
    if has_seq:
        in_specs += [kv_spec(s, HEAD_SLOT), kv_spec(s, V_HEAD)]
        args += [k, v]
    return pl.pallas_call(
        functools.partial(_attn_kernel, has_seq=has_seq),
        grid=(nb, nh, s // tq), in_specs=in_specs,
        out_specs=pl.BlockSpec((1, tq, V_HEAD), lambda b, h, i: (b, i, h)),
        out_shape=jax.ShapeDtypeStruct((nb, s, nh * V_HEAD), BF16),
        compiler_params=_cparams(3), name="attention" if has_seq else "attention_ctx",
    )(*args)


def _seqdft_kernel(c_ref, s_ref, fab_ref, o_ref, *, scale):
    y = jnp.dot(c_ref[...], fab_ref[0, 0], preferred_element_type=F32)
    y = y + jnp.dot(s_ref[...], fab_ref[0, 1], preferred_element_type=F32)
    o_ref[0] = (y * scale).astype(BF16)


def _seqdft(dft_c, dft_s, fab, group_w):
    nb, _, s, fw = fab.shape
    tr = min(s, 1024)
    scale = 1.0 / math.sqrt(s * group_w)
    return pl.pallas_call(
        functools.partial(_seqdft_kernel, scale=scale),
        grid=(s // tr, nb),
        in_specs=[pl.BlockSpec((tr, s), lambda i, b: (i, 0)),
                  pl.BlockSpec((tr, s), lambda i, b: (i, 0)),
                  pl.BlockSpec((1, 2, s, fw), lambda i, b: (b, 0, 0, 0))],
        out_specs=pl.BlockSpec((1, tr, fw), lambda i, b: (b, i, 0)),
        out_shape=jax.ShapeDtypeStruct((nb, s, fw), BF16),
        compiler_params=_cparams(2), name="seqdft",
    )(dft_c, dft_s, fab)


HALO = 16


def _mix_kernel(attn_ref, gb_ref, g_ref, gprev_ref, gnext_ref, fy_ref, x_ref, mod_ref, convw_ref,
                wout_ref, n2w_ref, wrcat_ref, wrhi_ref, br_ref, cntin_ref,
                x1_ref, h2_ref, route_ref, cnt_ref, carry_ref, *, tm, n_tiles, attn_w, conv_w):
    b, i = pl.program_id(0), pl.program_id(1)

    @pl.when((b == 0) & (i == 0))
    def _():
        carry_ref[...] = cntin_ref[...]

    g = g_ref[0].astype(F32)
    row = lax.broadcasted_iota(jnp.int32, g.shape, 0)
    prev_row = jnp.where(i > 0, gprev_ref[0, HALO - 1:HALO, :].astype(F32), 0.0)
    next_row = jnp.where(i < n_tiles - 1, gnext_ref[0, 0:1, :].astype(F32), 0.0)
    g_dn = jnp.where(row == 0, prev_row, pltpu.roll(g, 1, axis=0))
    g_up = jnp.where(row == tm - 1, next_row, pltpu.roll(g, tm - 1, axis=0))
    conv = gb_ref[0].astype(F32) * (g_dn * convw_ref[0:1, :] + g * convw_ref[1:2, :] + g_up * convw_ref[2:3, :])

    mix = jnp.dot(attn_ref[0], wout_ref[:attn_w, :], preferred_element_type=F32)
    mix = mix + jnp.dot(conv.astype(BF16), wout_ref[attn_w:attn_w + conv_w, :], preferred_element_type=F32)
    mix = mix + jnp.dot(fy_ref[0], wout_ref[attn_w + conv_w:, :], preferred_element_type=F32)
    x1 = x_ref[0] + mod_ref[0, 2:3, :] * mix
    x1_ref[0] = x1

    h2 = _rms(x1, n2w_ref[...]) * (1.0 + mod_ref[0, 4:5, :]) + mod_ref[0, 3:4, :]
    for r in range(ROW_CHUNKS):
        h2_ref[pl.ds(r, tm, stride=ROW_CHUNKS), :] = h2[:, r * LANES:(r + 1) * LANES]

    hi = h2.astype(BF16)
    lo = (h2 - hi.astype(F32)).astype(BF16)
    a = jnp.dot(hi, wrcat_ref[...], preferred_element_type=F32)
    logits = a[:, :LANES] + a[:, LANES:] + jnp.dot(lo, wrhi_ref[...], preferred_element_type=F32) + br_ref[...]

    lane = lax.broadcasted_iota(jnp.int32, logits.shape, 1)
    lane_f = lane.astype(F32)
    no_lane = float(LANES)
    is_group = lane < N_GROUPS
    gl = jnp.where(is_group, logits, NEG_BIG)
    gmax = jnp.max(gl, axis=-1, keepdims=True)
    gidx = jnp.min(jnp.where(gl == gmax, lane_f, no_lane), axis=-1, keepdims=True)
    g_p = 1.0 / jnp.sum(jnp.where(is_group, jnp.exp(gl - gmax), 0.0), axis=-1, keepdims=True)
    lane_group = ((lane - N_GROUPS) >> 3).astype(F32)
    in_group = (lane >= N_GROUPS) & (lane < N_GROUPS + N_EXPERTS) & (lane_group == gidx)
    el = jnp.where(in_group, logits, NEG_BIG)
    m1 = jnp.max(el, axis=-1, keepdims=True)
    i1 = jnp.min(jnp.where(el == m1, lane_f, no_lane), axis=-1, keepdims=True)
    el2 = jnp.where(lane_f == i1, NEG_BIG, el)
    m2 = jnp.max(el2, axis=-1, keepdims=True)
    i2 = jnp.min(jnp.where(el2 == m2, lane_f, no_lane), axis=-1, keepdims=True)
    e2 = jnp.exp(m2 - m1)
    w1 = g_p / (1.0 + e2)
    w2 = g_p * e2 / (1.0 + e2)

    hit1, hit2 = lane_f == i1, lane_f == i2
    onehot = jnp.where(hit1 | hit2, 1.0, 0.0)
    ti = lax.broadcasted_iota(jnp.int32, (tm, tm), 0)
    tj = lax.broadcasted_iota(jnp.int32, (tm, tm), 1)
    before = jnp.where(tj < ti, 1.0, 0.0).astype(BF16)
    seen = jnp.dot(before, onehot.astype(BF16), preferred_element_type=F32) + carry_ref[...]
    r1 = jnp.sum(jnp.where(hit1, seen, 0.0), axis=-1, keepdims=True)
    r2 = jnp.sum(jnp.where(hit2, seen, 0.0), axis=-1, keepdims=True)
    carry_ref[...] = carry_ref[...] + jnp.sum(onehot, axis=0, keepdims=True)
    cnt_ref[...] = carry_ref[...]

    route = jnp.where(lane == 0, i1 - N_GROUPS, 0.0)
    route = jnp.where(lane == 1, i2 - N_GROUPS, route)
    route = jnp.where(lane == 2, r1, route)
    route = jnp.where(lane == 3, r2, route)
    route = jnp.where(lane == 4, w1, route)
    route = jnp.where(lane == 5, w2, route)
    route_ref[...] = route


def _mix(attn, gb, g, fy, x, mods, lw, cnt_in):
    nb, s, d = x.shape
    tm = min(s, 256)
    n_tiles = s // tm
    attn_w, conv_w = attn.shape[-1], gb.shape[-1]
    tok = lambda w: pl.BlockSpec((1, tm, w), lambda b, i: (b, i, 0))
    hb = tm // HALO
    in_specs = [tok(attn_w), tok(conv_w), tok(conv_w),
                pl.BlockSpec((1, HALO, conv_w), lambda b, i: (b, jnp.maximum(i * hb - 1, 0), 0)),
                pl.BlockSpec((1, HALO, conv_w), lambda b, i: (b, jnp.minimum((i + 1) * hb, s // HALO - 1), 0)),
                tok(fy.shape[-1]), tok(d),
                pl.BlockSpec((1, 6, d), lambda b, i: (b, 0, 0))]
    weights = [lw["conv_w"], lw["w_out"], lw["n2w"], lw["wr_cat"], lw["wr_hi"], lw["b_r"], cnt_in]
    in_specs += [_const_spec(w.shape) for w in weights]
    n_tok = nb * s
    out_shape = [jax.ShapeDtypeStruct((nb, s, d), F32),
                 jax.ShapeDtypeStruct((n_tok * ROW_CHUNKS, LANES), F32),
                 jax.ShapeDtypeStruct((n_tok, LANES), F32),
                 jax.ShapeDtypeStruct((1, LANES), F32)]
    out_specs = [tok(d),
                 pl.BlockSpec((tm * ROW_CHUNKS, LANES), lambda b, i: (b * n_tiles + i, 0)),
                 pl.BlockSpec((tm, LANES), lambda b, i: (b * n_tiles + i, 0)),
                 pl.BlockSpec((1, LANES), lambda b, i: (0, 0))]
    return pl.pallas_call(
        functools.partial(_mix_kernel, tm=tm, n_tiles=n_tiles, attn_w=attn_w, conv_w=conv_w),
        grid=(nb, n_tiles), in_specs=in_specs, out_specs=out_specs, out_shape=out_shape,
        scratch_shapes=[pltpu.VMEM((1, LANES), F32)],
        compiler_params=_cparams(2), name="mix_router",
    )(attn, gb, g, g, g, fy, x, mods, *weights)


def _row(ref, idx):
    return ref.at[pl.ds(pl.multiple_of(idx * ROW_CHUNKS, ROW_CHUNKS), ROW_CHUNKS), :]


def _dispatch_kernel(slot_ref, *refs, tm, n_main_tiles, has_ctx):
    if has_ctx:
        h2_ref, h2c_ref, _, xs_ref, sem = refs
    else:
        h2_ref, _, xs_ref, sem = refs
    i = pl.program_id(0)

    def scatter(src_ref):
        def copies(j):
            return [pltpu.make_async_copy(_row(src_ref, j), _row(xs_ref, slot_ref[0, 0, TOP_K * j + k]), sem)
                    for k in range(TOP_K)]

        def start(j, carry):
            for c in copies(j):
                c.start()
            return carry

        def wait(j, carry):
            for c in copies(j):
                c.wait()
            return carry

        lax.fori_loop(0, tm, start, 0)
        lax.fori_loop(0, tm, wait, 0)

    if has_ctx:
        @pl.when(i < n_main_tiles)
        def _():
            scatter(h2_ref)

        @pl.when(i >= n_main_tiles)
        def _():
            scatter(h2c_ref)
    else:
        scatter(h2_ref)


def _dispatch(slots, h2, h2c, n_slots):
    tm = MOE_BLOCK
    n_main_tiles = h2.shape[0] // (tm * ROW_CHUNKS)
    has_ctx = h2c is not None
    n_tiles = slots.shape[0] // tm
    slots3 = slots.reshape(n_tiles, 1, tm * TOP_K)
    blk = (tm * ROW_CHUNKS, LANES)
    in_specs = [pl.BlockSpec((1, 1, tm * TOP_K), lambda i: (i, 0, 0), memory_space=pltpu.SMEM),
                pl.BlockSpec(blk, lambda i: (jnp.minimum(i, n_main_tiles - 1), 0))]
    args = [slots3, h2]
    if has_ctx:
        in_specs.append(pl.BlockSpec(blk, lambda i: (jnp.maximum(i - n_main_tiles, 0), 0)))
        args.append(h2c)
    in_specs.append(pl.BlockSpec(memory_space=pl.ANY))
    args.append(jnp.zeros((n_slots * ROW_CHUNKS, LANES), F32))
    return pl.pallas_call(
        functools.partial(_dispatch_kernel, tm=tm, n_main_tiles=n_main_tiles, has_ctx=has_ctx),
        grid=(n_tiles,), in_specs=in_specs,
        out_specs=pl.BlockSpec(memory_space=pl.ANY),
        out_shape=jax.ShapeDtypeStruct((n_slots * ROW_CHUNKS, LANES), F32),
        input_output_aliases={len(args) - 1: 0},
        scratch_shapes=[pltpu.SemaphoreType.DMA(())],
        compiler_params=_cparams(1), name="dispatch",
    )(*args)


def _expert_kernel(bexp_ref, nused_ref, xs_ref, wg_ref, wu_ref, wd_ref, y_ref, xb_ref, wgu_ref, wdb_ref, *, d_expert):
    b = pl.program_id(0)

    @pl.when(b >= nused_ref[0])
    def _():
        y_ref[...] = jnp.zeros(y_ref.shape, F32)

    @pl.when(b < nused_ref[0])
    def _():
        changed = jnp.logical_or(b == 0, bexp_ref[b] != bexp_ref[jnp.maximum(b - 1, 0)])

        @pl.when(changed)
        def _():
            wgu_ref[:, :d_expert] = wg_ref[0, 0].astype(BF16)
            wgu_ref[:, d_expert:] = wu_ref[0, 0].astype(BF16)
            wdb_ref[...] = wd_ref[0, 0].astype(BF16)

        for r in range(ROW_CHUNKS):
            xb_ref[:, r * LANES:(r + 1) * LANES] = xs_ref[pl.ds(r, MOE_BLOCK, stride=ROW_CHUNKS), :].astype(BF16)
        gu = jnp.dot(xb_ref[...], wgu_ref[...], preferred_element_type=F32)
        gate, up = gu[:, :d_expert], gu[:, d_expert:]
        hid = (gate * jax.nn.sigmoid(gate) * up).astype(BF16)
        y = jnp.dot(hid, wdb_ref[...], preferred_element_type=F32)
        for r in range(ROW_CHUNKS):
            y_ref[pl.ds(r, MOE_BLOCK, stride=ROW_CHUNKS), :] = y[:, r * LANES:(r + 1) * LANES]


def _experts(layer, block_exp, n_used, xs, w_gate, w_up, w_down):
    _, _, d, d_expert = w_gate.shape
    n_blocks = block_exp.shape[0]
    blk = (MOE_BLOCK * ROW_CHUNKS, LANES)
    live = lambda b, be, nu: (jnp.minimum(b, nu[0] - 1), 0)
    grid_spec = pltpu.PrefetchScalarGridSpec(
        num_scalar_prefetch=2, grid=(n_blocks,),
        in_specs=[pl.BlockSpec(blk, live),
                  pl.BlockSpec((1, 1, d, d_expert), lambda b, be, nu: (layer, be[b], 0, 0)),
                  pl.BlockSpec((1, 1, d, d_expert), lambda b, be, nu: (layer, be[b], 0, 0)),
                  pl.BlockSpec((1, 1, d_expert, d), lambda b, be, nu: (layer, be[b], 0, 0))],
        out_specs=pl.BlockSpec(blk, lambda b, be, nu: (b, 0)),
        scratch_shapes=[pltpu.VMEM((MOE_BLOCK, d), BF16),
                        pltpu.VMEM((d, 2 * d_expert), BF16),
                        pltpu.VMEM((d_expert, d), BF16)])
    return pl.pallas_call(
        functools.partial(_expert_kernel, d_expert=d_expert),
        grid_spec=grid_spec,
        out_shape=jax.ShapeDtypeStruct(xs.shape, F32),
        compiler_params=_cparams(1), name="experts",
    )(block_exp, n_used, xs, w_gate, w_up, w_down)


def _combine_kernel(slot_ref, x1_ref, mod_ref, route_ref, y_ref, o_ref, ybuf_ref, sem, *, tm):
    def copies(j):
        return [pltpu.make_async_copy(_row(y_ref, slot_ref[0, 0, TOP_K * j + k]), _row(ybuf_ref, k * tm + j), sem)
                for k in range(TOP_K)]

    def start(j, carry):
        for c in copies(j):
            c.start()
        return carry

    def wait(j, carry):
        for c in copies(j):
            c.wait()
        return carry

    lax.fori_loop(0, tm, start, 0)
    lax.fori_loop(0, tm, wait, 0)
    w1, w2 = route_ref[:, 4:5], route_ref[:, 5:6]
    for r in range(ROW_CHUNKS):
        ya = ybuf_ref[pl.ds(r, tm, stride=ROW_CHUNKS), :]
        yb = ybuf_ref[pl.ds(tm * ROW_CHUNKS + r, tm, stride=ROW_CHUNKS), :]
        lanes = slice(r * LANES, (r + 1) * LANES)
        o_ref[0, :, lanes] = x1_ref[0, :, lanes] + mod_ref[0, 5:6, lanes] * (ya * w1 + yb * w2)


def _combine(slots, x1, mods, route, y):
    nb, s, d = x1.shape
    tm = min(s, MOE_BLOCK)
    n_tiles = s // tm
    slots3 = slots.reshape(nb * n_tiles, 1, tm * TOP_K)
    return pl.pallas_call(
        functools.partial(_combine_kernel, tm=tm),
        grid=(nb, n_tiles),
        in_specs=[pl.BlockSpec((1, 1, tm * TOP_K), lambda b, i: (b * n_tiles + i, 0, 0), memory_space=pltpu.SMEM),
                  pl.BlockSpec((1, tm, d), lambda b, i: (b, i, 0)),
                  pl.BlockSpec((1, 6, d), lambda b, i: (b, 0, 0)),
                  pl.BlockSpec((tm, LANES), lambda b, i: (b * n_tiles + i, 0)),
                  pl.BlockSpec(memory_space=pl.ANY)],
        out_specs=pl.BlockSpec((1, tm, d), lambda b, i: (b, i, 0)),
        out_shape=jax.ShapeDtypeStruct((nb, s, d), F32),
        scratch_shapes=[pltpu.VMEM((TOP_K * tm * ROW_CHUNKS, LANES), F32), pltpu.SemaphoreType.DMA(())],
        compiler_params=_cparams(2), name="combine",
    )(slots3, x1, mods, route, y)


def _layer_weights(l, d, norm1_w, norm2_w, w_in, q_a_norm_w, w_uq, kv_a_norm_w, w_ukv, q_norm_w, k_norm_w,
                   conv_w, w_out, w_rg, b_rg, w_re, b_re):
    q_lora, kv_lora = q_a_norm_w.shape[1], kv_a_norm_w.shape[1]
    conv_cols = conv_w.shape[2]
    in_cols = w_in.shape[2]
    four_cols = in_cols - (q_lora + kv_lora + QK_ROPE + 3 * conv_cols)
    o_ckv = q_lora
    o_kr = o_ckv + kv_lora
    o_gb = o_kr + QK_ROPE
    o_gc, o_u, o_f = o_gb + conv_cols, o_gb + 2 * conv_cols, o_gb + 3 * conv_cols
    wi = w_in[l]
    w_in_p = jnp.concatenate([wi[:, :o_kr], wi[:, o_gb:], wi[:, o_kr:o_gb],
                              jnp.zeros((d, LANES - QK_ROPE), F32)], axis=1).astype(BF16)
    names, widths = ["cq", "ckv", "gb", "gc", "u", "f", "kr"], [q_lora, kv_lora, conv_cols, conv_cols, conv_cols, four_cols, LANES]
    cols, o = {}, 0
    for n, w in zip(names, widths):
        cols[n] = (o, o + w)
        o += w
    pad_head = lambda w: jnp.pad(w, [(0, 0)] * (w.ndim - 1) + [(0, HEAD_SLOT - QK_HEAD)])
    w_uq_p = pad_head(w_uq[l].reshape(q_lora, N_HEADS, QK_HEAD)).reshape(q_lora, N_HEADS * HEAD_SLOT).astype(BF16)
    w_ukv_p = (w_ukv[l].reshape(kv_lora, N_HEADS, 2, QK_NOPE).transpose(0, 2, 1, 3)
               .reshape(kv_lora, 2 * N_HEADS * QK_NOPE).astype(BF16))
    w_r = jnp.concatenate([w_rg[l], w_re[l], jnp.zeros((d, LANES - N_GROUPS - N_EXPERTS), F32)], axis=1)
    wr_hi = w_r.astype(BF16)
    wr_lo = (w_r - wr_hi.astype(F32)).astype(BF16)
    b_r = jnp.concatenate([b_rg[l], b_re[l], jnp.zeros((LANES - N_GROUPS - N_EXPERTS,), F32)]).reshape(1, LANES)
    return dict(cols=cols, n1w=norm1_w[l].reshape(1, d), n2w=norm2_w[l].reshape(1, d), w_in=w_in_p,
                qanw=q_a_norm_w[l].reshape(1, q_lora), w_uq=w_uq_p, kvanw=kv_a_norm_w[l].reshape(1, kv_lora),
                w_ukv=w_ukv_p, qnw=pad_head(q_norm_w[l]).reshape(1, HEAD_SLOT),
                knw=pad_head(k_norm_w[l]).reshape(1, HEAD_SLOT), conv_w=conv_w[l], w_out=w_out[l].astype(BF16),
                wr_cat=jnp.concatenate([wr_hi, wr_lo], axis=1), wr_hi=wr_hi, b_r=b_r)


def _rope_tables(n_tok):
    freqs = QK_ROPE // 4
    pos = jnp.arange(n_tok)
    row = (pos // GRID_W).astype(F32)
    colp = (pos % GRID_W).astype(F32)
    inv = ROPE_THETA ** (-jnp.arange(freqs, dtype=F32) / freqs)
    ar, ac = row[:, None] * inv, colp[:, None] * inv
    zeros = jnp.zeros((n_tok, LANES - QK_ROPE), F32)
    rope_c = jnp.concatenate([jnp.cos(ar), jnp.cos(ar), jnp.cos(ac), jnp.cos(ac), zeros], axis=1)
    rope_s = jnp.concatenate([-jnp.sin(ar), jnp.sin(ar), -jnp.sin(ac), jnp.sin(ac), zeros], axis=1)
    return rope_c, rope_s


def _identity_rope(n_tok):
    ones = jnp.concatenate([jnp.ones((n_tok, QK_ROPE), F32), jnp.zeros((n_tok, LANES - QK_ROPE), F32)], axis=1)
    return ones, jnp.zeros((n_tok, LANES), F32)


def _dft_tables(n):
    idx = jnp.arange(n, dtype=jnp.int32)
    ang = ((idx[:, None] * idx[None, :]) % n).astype(F32) * (2.0 * math.pi / n)
    return jnp.cos(ang), jnp.sin(ang)


def _slot_plan(routes, counts):
    n_tok = routes.shape[0]
    expert = routes[:, 0:TOP_K].astype(jnp.int32)
    rank = routes[:, TOP_K:2 * TOP_K].astype(jnp.int32)
    padded = (counts + MOE_BLOCK - 1) // MOE_BLOCK * MOE_BLOCK
    pad_ends = jnp.cumsum(padded)
    pad_starts = pad_ends - padded
    slots = pad_starts[expert] + rank
    n_blocks = -(-(n_tok * TOP_K) // MOE_BLOCK) + N_EXPERTS
    block_start = jnp.arange(n_blocks, dtype=jnp.int32) * MOE_BLOCK
    block_exp = jnp.minimum(jnp.searchsorted(pad_ends, block_start, side="right"), N_EXPERTS - 1).astype(jnp.int32)
    n_used = (pad_ends[-1] // MOE_BLOCK).astype(jnp.int32)
    last_exp = block_exp[jnp.maximum(n_used - 1, 0)]
    block_exp = jnp.where(jnp.arange(n_blocks) < n_used, block_exp, last_exp)
    return slots.astype(jnp.int32), block_exp, n_used.reshape(1), n_blocks * MOE_BLOCK


def kernel(x, c, ctx, c_ctx, w_mod, b_mod, norm1_w, norm2_w, w_in, q_a_norm_w, w_uq, kv_a_norm_w, w_ukv, q_norm_w, k_norm_w, conv_w, w_out, w_router_group, b_router_group, w_router_expert, b_router_expert, w_gate, w_up, w_down):
    nb, s, d = x.shape
    n_ctx = ctx.shape[1]
    depth = w_mod.shape[0]
    group_w = conv_w.shape[2] // FOURIER_GROUPS

    rows = -(-(nb + 1) // 8) * 8
    c_all = jnp.concatenate([c, c_ctx[None, :], jnp.zeros((rows - nb - 1, d), F32)], axis=0)
    mod_all = _modulation(c_all, w_mod, b_mod)

    rope_c, rope_s = _rope_tables(s)
    id_c, id_s = _identity_rope(n_ctx)
    cc, sc = _dft_tables(group_w)
    wdft = jnp.concatenate([cc, -sc], axis=1).astype(BF16)
    seq_c, seq_s = (t.astype(BF16) for t in _dft_tables(s))
    ctx_c, ctx_s = (t.astype(BF16) for t in _dft_tables(n_ctx))

    xc = ctx
    for l in range(depth):
        last = l == depth - 1
        lw = _layer_weights(l, d, norm1_w, norm2_w, w_in, q_a_norm_w, w_uq, kv_a_norm_w, w_ukv, q_norm_w,
                            k_norm_w, conv_w, w_out, w_router_group, b_router_group, w_router_expert,
                            b_router_expert)
        mods = mod_all[l, :nb].reshape(nb, 6, d)
        mods_c = jnp.broadcast_to(mod_all[l, nb].reshape(1, 6, d), (nb, 6, d))

        q, k, v, gb, g, fab = _inproj(x, mods, lw, rope_c, rope_s, wdft, kv_only=False)
        if last:
            kc, vc = _inproj(xc, mods_c, lw, id_c, id_s, wdft, kv_only=True)
        else:
            qc, kc, vc, gbc, gc, fabc = _inproj(xc, mods_c, lw, id_c, id_s, wdft, kv_only=False)
        attn = _attention(q, kc, vc, k, v)
        fy = _seqdft(seq_c, seq_s, fab, group_w)
        zero_cnt = jnp.zeros((1, LANES), F32)
        x1, h2, route, cnt = _mix(attn, gb, g, fy, x, mods, lw, zero_cnt)
        if not last:
            attn_c = _attention(qc, kc, vc)
            fyc = _seqdft(ctx_c, ctx_s, fabc, group_w)
            x1c, h2c, route_c, cnt = _mix(attn_c, gbc, gc, fyc, xc, mods_c, lw, cnt)
            routes = jnp.concatenate([route, route_c], axis=0)
        else:
            h2c, routes = None, route

        counts = cnt[0, N_GROUPS:N_GROUPS + N_EXPERTS].astype(jnp.int32)
        slots, block_exp, n_used, n_slots = _slot_plan(routes, counts)
        xs = _dispatch(slots, h2, h2c, n_slots)
        y = _experts(l, block_exp, n_used, xs, w_gate, w_up, w_down)
        n_main = nb * s
        x = _combine(slots[:n_main], x1, mods, route, y)
        if not last:
            xc = _combine(slots[n_main:], x1c, mods_c, route_c, y)
    return x
```

```python
import functools
import math

import jax
import jax.numpy as jnp
from jax import lax
from jax.experimental import pallas as pl
from jax.experimental.pallas import tpu as pltpu

F32 = jnp.float32
BF16 = jnp.bfloat16

EPS = 1e-6
N_HEADS = 8
QK_NOPE = 128
QK_ROPE = 64
QK_HEAD = QK_NOPE + QK_ROPE
V_HEAD = 128
GRID_W = 64
ROPE_THETA = 10000.0
ATTN_SCALE = QK_HEAD ** -0.5
Q_SCALE = ATTN_SCALE * math.log2(math.e)
CONV_K = 3
FOURIER_GROUPS = 4
N_GROUPS = 8
EXPERTS_PER_GROUP = 8
N_EXPERTS = N_GROUPS * EXPERTS_PER_GROUP
TOP_K = 2
MOE_BLOCK = 256

LANES = 128
HEAD_SLOT = 2 * LANES
ROW_CHUNKS = 16
VMEM_LIMIT = 56 * 1024 * 1024
NEG_BIG = -1e30


def _cparams(n_axes):
    return pltpu.CompilerParams(dimension_semantics=("arbitrary",) * n_axes,
                                vmem_limit_bytes=VMEM_LIMIT)


def _const_spec(shape):
    nd = len(shape)
    return pl.BlockSpec(shape, lambda *_: (0,) * nd)


def _mod_kernel(c_ref, w_ref, b_ref, o_ref):
    c = c_ref[...]
    a = (c * jax.nn.sigmoid(c)).astype(BF16)
    o_ref[0] = jnp.dot(a, w_ref[0].astype(BF16), preferred_element_type=F32) + b_ref[0]


def _modulation(c_all, w_mod, b_mod):
    n_layers, d, n_out = w_mod.shape
    rows = c_all.shape[0]
    tn = 1024
    return pl.pallas_call(
        _mod_kernel,
        grid=(n_layers, n_out // tn),
        in_specs=[pl.BlockSpec((rows, d), lambda l, j: (0, 0)),
                  pl.BlockSpec((1, d, tn), lambda l, j: (l, 0, j)),
                  pl.BlockSpec((1, 1, tn), lambda l, j: (l, 0, j))],
        out_specs=pl.BlockSpec((1, rows, tn), lambda l, j: (l, 0, j)),
        out_shape=jax.ShapeDtypeStruct((n_layers, rows, n_out), F32),
        compiler_params=_cparams(2),
        name="modulation",
    )(c_all, w_mod, b_mod.reshape(n_layers, 1, n_out))


def _rms(t, w):
    return t * lax.rsqrt(jnp.mean(t * t, axis=-1, keepdims=True) + EPS) * w


def _rope_partner(t):
    lane = lax.broadcasted_iota(jnp.int32, t.shape, 1)
    ahead = pltpu.roll(t, LANES - 16, axis=1)
    behind = pltpu.roll(t, 16, axis=1)
    return jnp.where((lane & 16) == 0, ahead, behind)


def _inproj_kernel(x_ref, mod_ref, n1w_ref, win_ref, qanw_ref, wuq_ref, kvanw_ref, wukv_ref,
                   qnw_ref, knw_ref, rc_ref, rs_ref, wdft_ref, *out_refs, cols, kv_only):
    if kv_only:
        k_ref, v_ref = out_refs
    else:
        q_ref, k_ref, v_ref, gb_ref, g_ref, fab_ref = out_refs
    x = x_ref[0]
    shift, scale = mod_ref[0, 0:1, :], mod_ref[0, 1:2, :]
    hb = (_rms(x, n1w_ref[...]) * (1.0 + scale) + shift).astype(BF16)
    rope_c, rope_s = rc_ref[...], rs_ref[...]

    def rope(t):
        return t * rope_c + _rope_partner(t) * rope_s

    if kv_only:
        def col(name):
            lo, hi = cols[name]
            return jnp.dot(hb, win_ref[:, lo:hi], preferred_element_type=F32)
    else:
        p = jnp.dot(hb, win_ref[...], preferred_element_type=F32)

        def col(name):
            lo, hi = cols[name]
            return p[:, lo:hi]

    ckv = _rms(col("ckv"), kvanw_ref[...]).astype(BF16)
    kv = jnp.dot(ckv, wukv_ref[...], preferred_element_type=F32)
    k_r = col("kr")
    ss_r = jnp.sum(k_r * k_r, axis=-1, keepdims=True)
    knw0, knw1 = knw_ref[:, :LANES], knw_ref[:, LANES:]
    kr_rot = rope(k_r * knw1)
    v_off = N_HEADS * QK_NOPE
    for h in range(N_HEADS):
        kn = kv[:, h * QK_NOPE:(h + 1) * QK_NOPE]
        r = lax.rsqrt((jnp.sum(kn * kn, axis=-1, keepdims=True) + ss_r) * (1.0 / QK_HEAD) + EPS)
        k_ref[0, h, :, :LANES] = (kn * r * knw0).astype(BF16)
        k_ref[0, h, :, LANES:] = (kr_rot * r).astype(BF16)
        v_ref[0, h] = kv[:, v_off + h * V_HEAD:v_off + (h + 1) * V_HEAD].astype(BF16)
    if kv_only:
        return

    cq = _rms(col("cq"), qanw_ref[...]).astype(BF16)
    qraw = jnp.dot(cq, wuq_ref[...], preferred_element_type=F32)
    qnw0, qnw1 = qnw_ref[:, :LANES], qnw_ref[:, LANES:]
    for h in range(N_HEADS):
        q0 = qraw[:, h * HEAD_SLOT:h * HEAD_SLOT + LANES]
        q1 = qraw[:, h * HEAD_SLOT + LANES:(h + 1) * HEAD_SLOT]
        ss = jnp.sum(q0 * q0, axis=-1, keepdims=True) + jnp.sum(q1 * q1, axis=-1, keepdims=True)
        r = lax.rsqrt(ss * (1.0 / QK_HEAD) + EPS) * Q_SCALE
        q_ref[0, h, :, :LANES] = (q0 * r * qnw0).astype(BF16)
        q_ref[0, h, :, LANES:] = rope(q1 * r * qnw1).astype(BF16)

    gb_ref[0] = col("gb").astype(BF16)
    g_ref[0] = (col("gc") * col("u")).astype(BF16)
    f = col("f").astype(BF16)
    for g in range(FOURIER_GROUPS):
        res = jnp.dot(f[:, g * LANES:(g + 1) * LANES], wdft_ref[...], preferred_element_type=F32)
        fab_ref[0, 0, :, g * LANES:(g + 1) * LANES] = res[:, :LANES].astype(BF16)
        fab_ref[0, 1, :, g * LANES:(g + 1) * LANES] = res[:, LANES:].astype(BF16)


def _inproj(x, mods, lw, rope_c, rope_s, wdft, *, kv_only):
    nb, s, d = x.shape
    tm = min(s, 256)
    cols = lw["cols"]
    fw = wdft.shape[0] * FOURIER_GROUPS
    cw = cols["gb"][1] - cols["gb"][0]
    kernel = functools.partial(_inproj_kernel, cols=cols, kv_only=kv_only)
    head_spec = lambda w: pl.BlockSpec((1, N_HEADS, tm, w), lambda b, i: (b, 0, i, 0))
    tok_spec = lambda w: pl.BlockSpec((1, tm, w), lambda b, i: (b, i, 0))
    out_shape = [jax.ShapeDtypeStruct((nb, N_HEADS, s, HEAD_SLOT), BF16),
                 jax.ShapeDtypeStruct((nb, N_HEADS, s, V_HEAD), BF16)]
    out_specs = [head_spec(HEAD_SLOT), head_spec(V_HEAD)]
    if not kv_only:
        out_shape = ([jax.ShapeDtypeStruct((nb, N_HEADS, s, HEAD_SLOT), BF16)] + out_shape +
                     [jax.ShapeDtypeStruct((nb, s, cw), BF16), jax.ShapeDtypeStruct((nb, s, cw), BF16),
                      jax.ShapeDtypeStruct((nb, 2, s, fw), BF16)])
        out_specs = ([head_spec(HEAD_SLOT)] + out_specs +
                     [tok_spec(cw), tok_spec(cw),
                      pl.BlockSpec((1, 2, tm, fw), lambda b, i: (b, 0, i, 0))])
    weights = [lw["n1w"], lw["w_in"], lw["qanw"], lw["w_uq"], lw["kvanw"], lw["w_ukv"], lw["qnw"], lw["knw"]]
    in_specs = ([pl.BlockSpec((1, tm, d), lambda b, i: (b, i, 0)),
                 pl.BlockSpec((1, 6, d), lambda b, i: (b, 0, 0))] +
                [_const_spec(w.shape) for w in weights] +
                [pl.BlockSpec((tm, LANES), lambda b, i: (i, 0)),
                 pl.BlockSpec((tm, LANES), lambda b, i: (i, 0)),
                 _const_spec(wdft.shape)])
    return pl.pallas_call(
        kernel, grid=(nb, s // tm), in_specs=in_specs, out_specs=out_specs, out_shape=out_shape,
        compiler_params=_cparams(2), name="inproj_kv" if kv_only else "inproj",
    )(x, mods, lw["n1w"], lw["w_in"], lw["qanw"], lw["w_uq"], lw["kvanw"], lw["w_ukv"], lw["qnw"],
      lw["knw"], rope_c, rope_s, wdft)


_NT = (((1,), (1,)), ((), ()))


def _attn_kernel(q_ref, kc_ref, vc_ref, *rest, has_seq, sub):
    o_ref = rest[-1]
    key_refs = [kc_ref] + ([rest[0]] if has_seq else [])
    val_refs = [vc_ref] + ([rest[1]] if has_seq else [])

    def scores(r0):
        q = q_ref[0, 0, r0:r0 + sub, :]
        return [lax.dot_general(q, k[0, 0], _NT, preferred_element_type=F32) for k in key_refs]

    def finish(r0, s_parts):
        m = functools.reduce(jnp.maximum, [jnp.max(s, axis=-1, keepdims=True) for s in s_parts])
        p_parts = [jnp.exp2(s - m) for s in s_parts]
        l = sum(jnp.sum(p, axis=-1, keepdims=True) for p in p_parts)
        o = sum(jnp.dot(p.astype(BF16), v[0, 0], preferred_element_type=F32) for p, v in zip(p_parts, val_refs))
        o_ref[0, r0:r0 + sub, :] = (o / l).astype(BF16)

    starts = list(range(0, q_ref.shape[2], sub))
    pending = scores(starts[0])
    for nxt in starts[1:] + [None]:
        upcoming = scores(nxt) if nxt is not None else None
        finish(nxt - sub if nxt is not None else starts[-1], pending)
        pending = upcoming


def _attention(q, kc, vc, k=None, v=None):
    nb, nh, s, _ = q.shape
    n_ctx = kc.shape[2]
    has_seq = k is not None
    tq = min(s, 2048)
    kv_spec = lambda n, w: pl.BlockSpec((1, 1, n, w), lambda b, h, i: (b, h, 0, 0))
    in_specs = [pl.BlockSpec((1, 1, tq, HEAD_SLOT), lambda b, h, i: (b, h, i, 0)),
                kv_spec(n_ctx, HEAD_SLOT), kv_spec(n_ctx, V_HEAD)]
    args = [q, kc, vc]
    if has_seq:
        in_specs += [kv_spec(s, HEAD_SLOT), kv_spec(s, V_HEAD)]
        args += [k, v]
    return pl.pallas_call(
        functools.partial(_attn_kernel, has_seq=has_seq, sub=min(tq, 256)),
        grid=(nb, nh, s // tq), in_specs=in_specs,
        out_specs=pl.BlockSpec((1, tq, V_HEAD), lambda b, h, i: (b, i, h)),
        out_shape=jax.ShapeDtypeStruct((nb, s, nh * V_HEAD), BF16),
        compiler_params=_cparams(3), name="attention" if has_seq else "attention_ctx",
    )(*args)


def _seqdft_kernel(c_ref, s_ref, fab_ref, o_ref, *, scale):
    y = jnp.dot(c_ref[...], fab_ref[0, 0], preferred_element_type=F32)
    y = y + jnp.dot(s_ref[...], fab_ref[0, 1], preferred_element_type=F32)
    o_ref[0] = (y * scale).astype(BF16)


def _seqdft(dft_c, dft_s, fab, group_w):
    nb, _, s, fw = fab.shape
    tr = min(s, 1024)
    scale = 1.0 / math.sqrt(s * group_w)
    return pl.pallas_call(
        functools.partial(_seqdft_kernel, scale=scale),
        grid=(s // tr, nb),
        in_specs=[pl.BlockSpec((tr, s), lambda i, b: (i, 0)),
                  pl.BlockSpec((tr, s), lambda i, b: (i, 0)),
                  pl.BlockSpec((1, 2, s, fw), lambda i, b: (b, 0, 0, 0))],
        out_specs=pl.BlockSpec((1, tr, fw), lambda i, b: (b, i, 0)),
        out_shape=jax.ShapeDtypeStruct((nb, s, fw), BF16),
        compiler_params=_cparams(2), name="seqdft",
    )(dft_c, dft_s, fab)


HALO = 16


def _mix_kernel(attn_ref, gb_ref, g_ref, gprev_ref, gnext_ref, fy_ref, x_ref, mod_ref, convw_ref,
                wout_ref, n2w_ref, wrcat_ref, wrhi_ref, br_ref, cntin_ref,
                x1_ref, h2_ref, route_ref, cnt_ref, carry_ref, *, tm, n_tiles, attn_w, conv_w):
    b, i = pl.program_id(0), pl.program_id(1)

    @pl.when((b == 0) & (i == 0))
    def _():
        carry_ref[...] = cntin_ref[...]

    g = g_ref[0].astype(F32)
    row = lax.broadcasted_iota(jnp.int32, g.shape, 0)
    prev_row = jnp.where(i > 0, gprev_ref[0, HALO - 1:HALO, :].astype(F32), 0.0)
    next_row = jnp.where(i < n_tiles - 1, gnext_ref[0, 0:1, :].astype(F32), 0.0)
    g_dn = jnp.where(row == 0, prev_row, pltpu.roll(g, 1, axis=0))
    g_up = jnp.where(row == tm - 1, next_row, pltpu.roll(g, tm - 1, axis=0))
    conv = gb_ref[0].astype(F32) * (g_dn * convw_ref[0:1, :] + g * convw_ref[1:2, :] + g_up * convw_ref[2:3, :])

    mix = jnp.dot(attn_ref[0], wout_ref[:attn_w, :], preferred_element_type=F32)
    mix = mix + jnp.dot(conv.astype(BF16), wout_ref[attn_w:attn_w + conv_w, :], preferred_element_type=F32)
    mix = mix + jnp.dot(fy_ref[0], wout_ref[attn_w + conv_w:, :], preferred_element_type=F32)
    x1 = x_ref[0] + mod_ref[0, 2:3, :] * mix
    x1_ref[0] = x1

    h2 = _rms(x1, n2w_ref[...]) * (1.0 + mod_ref[0, 4:5, :]) + mod_ref[0, 3:4, :]
    for r in range(ROW_CHUNKS):
        h2_ref[pl.ds(r, tm, stride=ROW_CHUNKS), :] = h2[:, r * LANES:(r + 1) * LANES]

    hi = h2.astype(BF16)
    lo = (h2 - hi.astype(F32)).astype(BF16)
    a = jnp.dot(hi, wrcat_ref[...], preferred_element_type=F32)
    logits = a[:, :LANES] + a[:, LANES:] + jnp.dot(lo, wrhi_ref[...], preferred_element_type=F32) + br_ref[...]

    lane = lax.broadcasted_iota(jnp.int32, logits.shape, 1)
    lane_f = lane.astype(F32)
    no_lane = float(LANES)
    is_group = lane < N_GROUPS
    gl = jnp.where(is_group, logits, NEG_BIG)
    gmax = jnp.max(gl, axis=-1, keepdims=True)
    gidx = jnp.min(jnp.where(gl == gmax, lane_f, no_lane), axis=-1, keepdims=True)
    g_p = 1.0 / jnp.sum(jnp.where(is_group, jnp.exp(gl - gmax), 0.0), axis=-1, keepdims=True)
    lane_group = ((lane - N_GROUPS) >> 3).astype(F32)
    in_group = (lane >= N_GROUPS) & (lane < N_GROUPS + N_EXPERTS) & (lane_group == gidx)
    el = jnp.where(in_group, logits, NEG_BIG)
    m1 = jnp.max(el, axis=-1, keepdims=True)
    i1 = jnp.min(jnp.where(el == m1, lane_f, no_lane), axis=-1, keepdims=True)
    el2 = jnp.where(lane_f == i1, NEG_BIG, el)
    m2 = jnp.max(el2, axis=-1, keepdims=True)
    i2 = jnp.min(jnp.where(el2 == m2, lane_f, no_lane), axis=-1, keepdims=True)
    e2 = jnp.exp(m2 - m1)
    w1 = g_p / (1.0 + e2)
    w2 = g_p * e2 / (1.0 + e2)

    hit1, hit2 = lane_f == i1, lane_f == i2
    onehot = jnp.where(hit1 | hit2, 1.0, 0.0)
    ti = lax.broadcasted_iota(jnp.int32, (tm, tm), 0)
    tj = lax.broadcasted_iota(jnp.int32, (tm, tm), 1)
    before = jnp.where(tj < ti, 1.0, 0.0).astype(BF16)
    seen = jnp.dot(before, onehot.astype(BF16), preferred_element_type=F32) + carry_ref[...]
    r1 = jnp.sum(jnp.where(hit1, seen, 0.0), axis=-1, keepdims=True)
    r2 = jnp.sum(jnp.where(hit2, seen, 0.0), axis=-1, keepdims=True)
    carry_ref[...] = carry_ref[...] + jnp.sum(onehot, axis=0, keepdims=True)
    cnt_ref[...] = carry_ref[...]

    route = jnp.where(lane == 0, i1 - N_GROUPS, 0.0)
    route = jnp.where(lane == 1, i2 - N_GROUPS, route)
    route = jnp.where(lane == 2, r1, route)
    route = jnp.where(lane == 3, r2, route)
    route = jnp.where(lane == 4, w1, route)
    route = jnp.where(lane == 5, w2, route)
    route_ref[...] = route


def _mix(attn, gb, g, fy, x, mods, lw, cnt_in):
    nb, s, d = x.shape
    tm = min(s, 256)
    n_tiles = s // tm
    attn_w, conv_w = attn.shape[-1], gb.shape[-1]
    tok = lambda w: pl.BlockSpec((1, tm, w), lambda b, i: (b, i, 0))
    hb = tm // HALO
    in_specs = [tok(attn_w), tok(conv_w), tok(conv_w),
                pl.BlockSpec((1, HALO, conv_w), lambda b, i: (b, jnp.maximum(i * hb - 1, 0), 0)),
                pl.BlockSpec((1, HALO, conv_w), lambda b, i: (b, jnp.minimum((i + 1) * hb, s // HALO - 1), 0)),
                tok(fy.shape[-1]), tok(d),
                pl.BlockSpec((1, 6, d), lambda b, i: (b, 0, 0))]
    weights = [lw["conv_w"], lw["w_out"], lw["n2w"], lw["wr_cat"], lw["wr_hi"], lw["b_r"], cnt_in]
    in_specs += [_const_spec(w.shape) for w in weights]
    n_tok = nb * s
    out_shape = [jax.ShapeDtypeStruct((nb, s, d), F32),
                 jax.ShapeDtypeStruct((n_tok * ROW_CHUNKS, LANES), F32),
                 jax.ShapeDtypeStruct((n_tok, LANES), F32),
                 jax.ShapeDtypeStruct((1, LANES), F32)]
    out_specs = [tok(d),
                 pl.BlockSpec((tm * ROW_CHUNKS, LANES), lambda b, i: (b * n_tiles + i, 0)),
                 pl.BlockSpec((tm, LANES), lambda b, i: (b * n_tiles + i, 0)),
                 pl.BlockSpec((1, LANES), lambda b, i: (0, 0))]
    return pl.pallas_call(
        functools.partial(_mix_kernel, tm=tm, n_tiles=n_tiles, attn_w=attn_w, conv_w=conv_w),
        grid=(nb, n_tiles), in_specs=in_specs, out_specs=out_specs, out_shape=out_shape,
        scratch_shapes=[pltpu.VMEM((1, LANES), F32)],
        compiler_params=_cparams(2), name="mix_router",
    )(attn, gb, g, g, g, fy, x, mods, *weights)


def _row(ref, idx):
    return ref.at[pl.ds(pl.multiple_of(idx * ROW_CHUNKS, ROW_CHUNKS), ROW_CHUNKS), :]


DMA_UNROLL = 8
ZERO_ROWS = 128


def _row_dma_loops(n_rows, copies):
    def run(method):
        def body(jo, carry):
            for u in range(DMA_UNROLL):
                for c in copies(jo * DMA_UNROLL + u):
                    getattr(c, method)()
            return carry
        lax.fori_loop(0, n_rows // DMA_UNROLL, body, 0)
    run("start")
    run("wait")


def _dispatch_kernel(lo_ref, n_ref, slot_ref, *refs, tm, n_main_tiles, has_ctx):
    if has_ctx:
        h2_ref, h2c_ref, xs_ref, zero_ref, sem, zsem = refs
    else:
        h2_ref, xs_ref, zero_ref, sem, zsem = refs
    i = pl.program_id(0)

    @pl.when(i == 0)
    def _():
        zero_ref[...] = jnp.zeros(zero_ref.shape, F32)

        def zero_copy(first, rows):
            return pltpu.make_async_copy(
                zero_ref.at[pl.ds(0, rows * ROW_CHUNKS), :],
                xs_ref.at[pl.ds(pl.multiple_of(first * ROW_CHUNKS, ROW_CHUNKS), rows * ROW_CHUNKS), :], zsem)

        def fill(e, carry):
            lo, n = lo_ref[e], n_ref[e]
            n_full = n // ZERO_ROWS

            def run(method):
                def full(c, cc):
                    getattr(zero_copy(lo + c * ZERO_ROWS, ZERO_ROWS), method)()
                    return cc
                lax.fori_loop(0, n_full, full, 0)
                first = lo + n_full * ZERO_ROWS
                p = ZERO_ROWS // 2
                while p >= 1:
                    @pl.when((n & p) != 0)
                    def _(first=first, p=p):
                        getattr(zero_copy(first, p), method)()
                    first = first + (n & p)
                    p //= 2
            run("start")
            run("wait")
            return carry

        lax.fori_loop(0, lo_ref.shape[0], fill, 0)

    def scatter(src_ref):
        _row_dma_loops(tm, lambda j: [
            pltpu.make_async_copy(_row(src_ref, j), _row(xs_ref, slot_ref[0, 0, TOP_K * j + k]), sem)
            for k in range(TOP_K)])

    if has_ctx:
        @pl.when(i < n_main_tiles)
        def _():
            scatter(h2_ref)

        @pl.when(i >= n_main_tiles)
        def _():
            scatter(h2c_ref)
    else:
        scatter(h2_ref)


def _dispatch(slots, fill_lo, fill_n, h2, h2c, n_slots):
    tm = MOE_BLOCK
    n_main_tiles = h2.shape[0] // (tm * ROW_CHUNKS)
    has_ctx = h2c is not None
    n_tiles = slots.shape[0] // tm
    slots3 = slots.reshape(n_tiles, 1, tm * TOP_K)
    blk = (tm * ROW_CHUNKS, LANES)
    in_specs = [pl.BlockSpec((1, 1, tm * TOP_K), lambda i, lo, n: (i, 0, 0), memory_space=pltpu.SMEM),
                pl.BlockSpec(blk, lambda i, lo, n: (jnp.minimum(i, n_main_tiles - 1), 0))]
    args = [slots3, h2]
    if has_ctx:
        in_specs.append(pl.BlockSpec(blk, lambda i, lo, n: (jnp.maximum(i - n_main_tiles, 0), 0)))
        args.append(h2c)
    grid_spec = pltpu.PrefetchScalarGridSpec(
        num_scalar_prefetch=2, grid=(n_tiles,), in_specs=in_specs,
        out_specs=pl.BlockSpec(memory_space=pl.ANY),
        scratch_shapes=[pltpu.VMEM((ZERO_ROWS * ROW_CHUNKS, LANES), F32),
                        pltpu.SemaphoreType.DMA(()), pltpu.SemaphoreType.DMA(())])
    return pl.pallas_call(
        functools.partial(_dispatch_kernel, tm=tm, n_main_tiles=n_main_tiles, has_ctx=has_ctx),
        grid_spec=grid_spec,
        out_shape=jax.ShapeDtypeStruct((n_slots * ROW_CHUNKS, LANES), F32),
        compiler_params=_cparams(1), name="dispatch",
    )(fill_lo, fill_n, *args)


def _expert_kernel(bexp_ref, nused_ref, xs_ref, wg_ref, wu_ref, wd_ref, y_ref, xb_ref, wgu_ref, wdb_ref, *, d_expert):
    b = pl.program_id(0)

    @pl.when(b >= nused_ref[0])
    def _():
        y_ref[...] = jnp.zeros(y_ref.shape, F32)

    @pl.when(b < nused_ref[0])
    def _():
        changed = jnp.logical_or(b == 0, bexp_ref[b] != bexp_ref[jnp.maximum(b - 1, 0)])

        @pl.when(changed)
        def _():
            wgu_ref[:, :d_expert] = wg_ref[0, 0].astype(BF16)
            wgu_ref[:, d_expert:] = wu_ref[0, 0].astype(BF16)
            wdb_ref[...] = wd_ref[0, 0].astype(BF16)

        for r in range(ROW_CHUNKS):
            xb_ref[:, r * LANES:(r + 1) * LANES] = xs_ref[pl.ds(r, MOE_BLOCK, stride=ROW_CHUNKS), :].astype(BF16)
        gu = jnp.dot(xb_ref[...], wgu_ref[...], preferred_element_type=F32)
        gate, up = gu[:, :d_expert], gu[:, d_expert:]
        hid = (gate * jax.nn.sigmoid(gate) * up).astype(BF16)
        y = jnp.dot(hid, wdb_ref[...], preferred_element_type=F32)
        for r in range(ROW_CHUNKS):
            y_ref[pl.ds(r, MOE_BLOCK, stride=ROW_CHUNKS), :] = y[:, r * LANES:(r + 1) * LANES]


def _experts(layer, block_exp, n_used, xs, w_gate, w_up, w_down):
    _, _, d, d_expert = w_gate.shape
    n_blocks = block_exp.shape[0]
    blk = (MOE_BLOCK * ROW_CHUNKS, LANES)
    live = lambda b, be, nu: (jnp.minimum(b, nu[0] - 1), 0)
    grid_spec = pltpu.PrefetchScalarGridSpec(
        num_scalar_prefetch=2, grid=(n_blocks,),
        in_specs=[pl.BlockSpec(blk, live),
                  pl.BlockSpec((1, 1, d, d_expert), lambda b, be, nu: (layer, be[b], 0, 0)),
                  pl.BlockSpec((1, 1, d, d_expert), lambda b, be, nu: (layer, be[b], 0, 0)),
                  pl.BlockSpec((1, 1, d_expert, d), lambda b, be, nu: (layer, be[b], 0, 0))],
        out_specs=pl.BlockSpec(blk, lambda b, be, nu: (b, 0)),
        scratch_shapes=[pltpu.VMEM((MOE_BLOCK, d), BF16),
                        pltpu.VMEM((d, 2 * d_expert), BF16),
                        pltpu.VMEM((d_expert, d), BF16)])
    return pl.pallas_call(
        functools.partial(_expert_kernel, d_expert=d_expert),
        grid_spec=grid_spec,
        out_shape=jax.ShapeDtypeStruct(xs.shape, F32),
        compiler_params=_cparams(1), name="experts",
    )(block_exp, n_used, xs, w_gate, w_up, w_down)


def _combine_kernel(slot_ref, x1_ref, mod_ref, route_ref, y_ref, o_ref, ybuf_ref, sem, *, tm):
    _row_dma_loops(tm, lambda j: [
        pltpu.make_async_copy(_row(y_ref, slot_ref[0, 0, TOP_K * j + k]), _row(ybuf_ref, k * tm + j), sem)
        for k in range(TOP_K)])
    w1, w2 = route_ref[:, 4:5], route_ref[:, 5:6]
    for r in range(ROW_CHUNKS):
        ya = ybuf_ref[pl.ds(r, tm, stride=ROW_CHUNKS), :]
        yb = ybuf_ref[pl.ds(tm * ROW_CHUNKS + r, tm, stride=ROW_CHUNKS), :]
        lanes = slice(r * LANES, (r + 1) * LANES)
        o_ref[0, :, lanes] = x1_ref[0, :, lanes] + mod_ref[0, 5:6, lanes] * (ya * w1 + yb * w2)


def _combine(slots, x1, mods, route, y):
    nb, s, d = x1.shape
    tm = min(s, MOE_BLOCK)
    n_tiles = s // tm
    slots3 = slots.reshape(nb * n_tiles, 1, tm * TOP_K)
    return pl.pallas_call(
        functools.partial(_combine_kernel, tm=tm),
        grid=(nb, n_tiles),
        in_specs=[pl.BlockSpec((1, 1, tm * TOP_K), lambda b, i: (b * n_tiles + i, 0, 0), memory_space=pltpu.SMEM),
                  pl.BlockSpec((1, tm, d), lambda b, i: (b, i, 0)),
                  pl.BlockSpec((1, 6, d), lambda b, i: (b, 0, 0)),
                  pl.BlockSpec((tm, LANES), lambda b, i: (b * n_tiles + i, 0)),
                  pl.BlockSpec(memory_space=pl.ANY)],
        out_specs=pl.BlockSpec((1, tm, d), lambda b, i: (b, i, 0)),
        out_shape=jax.ShapeDtypeStruct((nb, s, d), F32),
        scratch_shapes=[pltpu.VMEM((TOP_K * tm * ROW_CHUNKS, LANES), F32), pltpu.SemaphoreType.DMA(())],
        compiler_params=_cparams(2), name="combine",
    )(slots3, x1, mods, route, y)


def _layer_weights(l, d, norm1_w, norm2_w, w_in, q_a_norm_w, w_uq, kv_a_norm_w, w_ukv, q_norm_w, k_norm_w,
                   conv_w, w_out, w_rg, b_rg, w_re, b_re):
    q_lora, kv_lora = q_a_norm_w.shape[1], kv_a_norm_w.shape[1]
    conv_cols = conv_w.shape[2]
    in_cols = w_in.shape[2]
    four_cols = in_cols - (q_lora + kv_lora + QK_ROPE + 3 * conv_cols)
    o_ckv = q_lora
    o_kr = o_ckv + kv_lora
    o_gb = o_kr + QK_ROPE
    o_gc, o_u, o_f = o_gb + conv_cols, o_gb + 2 * conv_cols, o_gb + 3 * conv_cols
    wi = w_in[l]
    w_in_p = jnp.concatenate([wi[:, :o_kr], wi[:, o_gb:], wi[:, o_kr:o_gb],
                              jnp.zeros((d, LANES - QK_ROPE), F32)], axis=1).astype(BF16)
    names, widths = ["cq", "ckv", "gb", "gc", "u", "f", "kr"], [q_lora, kv_lora, conv_cols, conv_cols, conv_cols, four_cols, LANES]
    cols, o = {}, 0
    for n, w in zip(names, widths):
        cols[n] = (o, o + w)
        o += w
    pad_head = lambda w: jnp.pad(w, [(0, 0)] * (w.ndim - 1) + [(0, HEAD_SLOT - QK_HEAD)])
    w_uq_p = pad_head(w_uq[l].reshape(q_lora, N_HEADS, QK_HEAD)).reshape(q_lora, N_HEADS * HEAD_SLOT).astype(BF16)
    w_ukv_p = (w_ukv[l].reshape(kv_lora, N_HEADS, 2, QK_NOPE).transpose(0, 2, 1, 3)
               .reshape(kv_lora, 2 * N_HEADS * QK_NOPE).astype(BF16))
    w_r = jnp.concatenate([w_rg[l], w_re[l], jnp.zeros((d, LANES - N_GROUPS - N_EXPERTS), F32)], axis=1)
    wr_hi = w_r.astype(BF16)
    wr_lo = (w_r - wr_hi.astype(F32)).astype(BF16)
    b_r = jnp.concatenate([b_rg[l], b_re[l], jnp.zeros((LANES - N_GROUPS - N_EXPERTS,), F32)]).reshape(1, LANES)
    return dict(cols=cols, n1w=norm1_w[l].reshape(1, d), n2w=norm2_w[l].reshape(1, d), w_in=w_in_p,
                qanw=q_a_norm_w[l].reshape(1, q_lora), w_uq=w_uq_p, kvanw=kv_a_norm_w[l].reshape(1, kv_lora),
                w_ukv=w_ukv_p, qnw=pad_head(q_norm_w[l]).reshape(1, HEAD_SLOT),
                knw=pad_head(k_norm_w[l]).reshape(1, HEAD_SLOT), conv_w=conv_w[l], w_out=w_out[l].astype(BF16),
                wr_cat=jnp.concatenate([wr_hi, wr_lo], axis=1), wr_hi=wr_hi, b_r=b_r)


def _rope_tables(n_tok):
    freqs = QK_ROPE // 4
    pos = jnp.arange(n_tok)
    row = (pos // GRID_W).astype(F32)
    colp = (pos % GRID_W).astype(F32)
    inv = ROPE_THETA ** (-jnp.arange(freqs, dtype=F32) / freqs)
    ar, ac = row[:, None] * inv, colp[:, None] * inv
    zeros = jnp.zeros((n_tok, LANES - QK_ROPE), F32)
    rope_c = jnp.concatenate([jnp.cos(ar), jnp.cos(ar), jnp.cos(ac), jnp.cos(ac), zeros], axis=1)
    rope_s = jnp.concatenate([-jnp.sin(ar), jnp.sin(ar), -jnp.sin(ac), jnp.sin(ac), zeros], axis=1)
    return rope_c, rope_s


def _identity_rope(n_tok):
    ones = jnp.concatenate([jnp.ones((n_tok, QK_ROPE), F32), jnp.zeros((n_tok, LANES - QK_ROPE), F32)], axis=1)
    return ones, jnp.zeros((n_tok, LANES), F32)


def _dft_tables(n):
    idx = jnp.arange(n, dtype=jnp.int32)
    ang = ((idx[:, None] * idx[None, :]) % n).astype(F32) * (2.0 * math.pi / n)
    return jnp.cos(ang), jnp.sin(ang)


def _slot_plan(routes, counts):
    n_tok = routes.shape[0]
    expert = routes[:, 0:TOP_K].astype(jnp.int32)
    rank = routes[:, TOP_K:2 * TOP_K].astype(jnp.int32)
    padded = (counts + MOE_BLOCK - 1) // MOE_BLOCK * MOE_BLOCK
    pad_ends = jnp.cumsum(padded)
    pad_starts = pad_ends - padded
    slots = pad_starts[expert] + rank
    n_blocks = -(-(n_tok * TOP_K) // MOE_BLOCK) + N_EXPERTS
    n_slots = n_blocks * MOE_BLOCK
    block_start = jnp.arange(n_blocks, dtype=jnp.int32) * MOE_BLOCK
    block_exp = jnp.sum((pad_ends[None, :] <= block_start[:, None]).astype(jnp.int32), axis=1)
    block_exp = jnp.minimum(block_exp, N_EXPERTS - 1)
    n_used = (pad_ends[-1] // MOE_BLOCK).astype(jnp.int32)
    last_exp = block_exp[jnp.maximum(n_used - 1, 0)]
    block_exp = jnp.where(jnp.arange(n_blocks) < n_used, block_exp, last_exp)
    fill_lo = jnp.concatenate([pad_starts + counts, pad_ends[-1:]]).astype(jnp.int32)
    fill_n = jnp.concatenate([padded - counts, n_slots - pad_ends[-1:]]).astype(jnp.int32)
    return slots.astype(jnp.int32), block_exp, n_used.reshape(1), fill_lo, fill_n, n_slots


def kernel(x, c, ctx, c_ctx, w_mod, b_mod, norm1_w, norm2_w, w_in, q_a_norm_w, w_uq, kv_a_norm_w, w_ukv, q_norm_w, k_norm_w, conv_w, w_out, w_router_group, b_router_group, w_router_expert, b_router_expert, w_gate, w_up, w_down):
    nb, s, d = x.shape
    n_ctx = ctx.shape[1]
    depth = w_mod.shape[0]
    group_w = conv_w.shape[2] // FOURIER_GROUPS

    rows = -(-(nb + 1) // 8) * 8
    c_all = jnp.concatenate([c, c_ctx[None, :], jnp.zeros((rows - nb - 1, d), F32)], axis=0)
    mod_all = _modulation(c_all, w_mod, b_mod)

    rope_c, rope_s = _rope_tables(s)
    id_c, id_s = _identity_rope(n_ctx)
    cc, sc = _dft_tables(group_w)
    wdft = jnp.concatenate([cc, -sc], axis=1).astype(BF16)
    seq_c, seq_s = (t.astype(BF16) for t in _dft_tables(s))
    ctx_c, ctx_s = (t.astype(BF16) for t in _dft_tables(n_ctx))

    xc = ctx
    for l in range(depth):
        last = l == depth - 1
        lw = _layer_weights(l, d, norm1_w, norm2_w, w_in, q_a_norm_w, w_uq, kv_a_norm_w, w_ukv, q_norm_w,
                            k_norm_w, conv_w, w_out, w_router_group, b_router_group, w_router_expert,
                            b_router_expert)
        mods = mod_all[l, :nb].reshape(nb, 6, d)
        mods_c = jnp.broadcast_to(mod_all[l, nb].reshape(1, 6, d), (nb, 6, d))

        q, k, v, gb, g, fab = _inproj(x, mods, lw, rope_c, rope_s, wdft, kv_only=False)
        if last:
            kc, vc = _inproj(xc, mods_c, lw, id_c, id_s, wdft, kv_only=True)
        else:
            qc, kc, vc, gbc, gc, fabc = _inproj(xc, mods_c, lw, id_c, id_s, wdft, kv_only=False)
        attn = _attention(q, kc, vc, k, v)
        fy = _seqdft(seq_c, seq_s, fab, group_w)
        zero_cnt = jnp.zeros((1, LANES), F32)
        x1, h2, route, cnt = _mix(attn, gb, g, fy, x, mods, lw, zero_cnt)
        if not last:
            attn_c = _attention(qc, kc, vc)
            fyc = _seqdft(ctx_c, ctx_s, fabc, group_w)
            x1c, h2c, route_c, cnt = _mix(attn_c, gbc, gc, fyc, xc, mods_c, lw, cnt)
            routes = jnp.concatenate([route, route_c], axis=0)
        else:
            h2c, routes = None, route

        counts = cnt[0, N_GROUPS:N_GROUPS + N_EXPERTS].astype(jnp.int32)
        slots, block_exp, n_used, fill_lo, fill_n, n_slots = _slot_plan(routes, counts)
        xs = _dispatch(slots, fill_lo, fill_n, h2, h2c, n_slots)
        y = _experts(l, block_exp, n_used, xs, w_gate, w_up, w_down)
        n_main = nb * s
        x = _combine(slots[:n_main], x1, mods, route, y)
        if not last:
            xc = _combine(slots[n_main:], x1c, mods_c, route_c, y)
    return x
```

```python
import functools
import math

import jax
import jax.numpy as jnp
from jax import lax
from jax.experimental import pallas as pl
from jax.experimental.pallas import tpu as pltpu

F32 = jnp.float32
BF16 = jnp.bfloat16

EPS = 1e-6
N_HEADS = 8
QK_NOPE = 128
QK_ROPE = 64
QK_HEAD = QK_NOPE + QK_ROPE
V_HEAD = 128
GRID_W = 64
ROPE_THETA = 10000.0
ATTN_SCALE = QK_HEAD ** -0.5
Q_SCALE = ATTN_SCALE * math.log2(math.e)
CONV_K = 3
FOURIER_GROUPS = 4
N_GROUPS = 8
EXPERTS_PER_GROUP = 8
N_EXPERTS = N_GROUPS * EXPERTS_PER_GROUP
TOP_K = 2
MOE_BLOCK = 256

LANES = 128
HEAD_SLOT = 2 * LANES
WORD_ROWS = 8
ROUTE_ROWS = 8
U32 = jnp.uint32
VMEM_LIMIT = 56 * 1024 * 1024
NEG_BIG = -1e30


def _cparams(n_axes):
    return pltpu.CompilerParams(dimension_semantics=("arbitrary",) * n_axes,
                                vmem_limit_bytes=VMEM_LIMIT)


def _pack_pair(lo, hi):
    lo_bits = lax.bitcast_convert_type(lo.astype(BF16).astype(F32), U32) >> 16
    hi_bits = lax.bitcast_convert_type(hi.astype(BF16).astype(F32), U32) & U32(0xFFFF0000)
    return hi_bits | lo_bits


def _unpack_pair(words):
    lo = lax.bitcast_convert_type(words << 16, F32)
    hi = lax.bitcast_convert_type(words & U32(0xFFFF0000), F32)
    return lo, hi


def _pack_rows(dst_ref, val, row0=0):
    rows, d = val.shape
    half = d // 2
    for r in range(WORD_ROWS):
        dst_ref[pl.ds(row0 * WORD_ROWS + r, rows, stride=WORD_ROWS), :] = _pack_pair(
            val[:, r * LANES:(r + 1) * LANES], val[:, half + r * LANES:half + (r + 1) * LANES])


def _const_spec(shape):
    nd = len(shape)
    return pl.BlockSpec(shape, lambda *_: (0,) * nd, pipeline_mode=pl.Buffered(1))


def _mod_kernel(c_ref, w_ref, b_ref, o_ref):
    c = c_ref[...]
    a = (c * jax.nn.sigmoid(c)).astype(BF16)
    o_ref[0] = jnp.dot(a, w_ref[0].astype(BF16), preferred_element_type=F32) + b_ref[0]


def _modulation(c_all, w_mod, b_mod):
    n_layers, d, n_out = w_mod.shape
    rows = c_all.shape[0]
    tn = 1024
    return pl.pallas_call(
        _mod_kernel,
        grid=(n_layers, n_out // tn),
        in_specs=[pl.BlockSpec((rows, d), lambda l, j: (0, 0)),
                  pl.BlockSpec((1, d, tn), lambda l, j: (l, 0, j)),
                  pl.BlockSpec((1, 1, tn), lambda l, j: (l, 0, j))],
        out_specs=pl.BlockSpec((1, rows, tn), lambda l, j: (l, 0, j)),
        out_shape=jax.ShapeDtypeStruct((n_layers, rows, n_out), F32),
        compiler_params=_cparams(2),
        name="modulation",
    )(c_all, w_mod, b_mod.reshape(n_layers, 1, n_out))


def _rms(t, w):
    return t * lax.rsqrt(jnp.mean(t * t, axis=-1, keepdims=True) + EPS) * w


def _rope_partner(t):
    lane = lax.broadcasted_iota(jnp.int32, t.shape, 1)
    ahead = pltpu.roll(t, LANES - 16, axis=1)
    behind = pltpu.roll(t, 16, axis=1)
    return jnp.where((lane & 16) == 0, ahead, behind)


def _inproj_kernel(x_ref, mod_ref, n1w_ref, win_ref, qanw_ref, wuq_ref, kvanw_ref, wukv_ref,
                   qnw_ref, knw_ref, rc_ref, rs_ref, wdft_ref, *out_refs, cols, kv_only, sub):
    if kv_only:
        k_ref, v_ref = out_refs
    else:
        q_ref, k_ref, v_ref, gb_ref, g_ref, fab_ref = out_refs
    shift, scale = mod_ref[0, 0:1, :], mod_ref[0, 1:2, :]

    def project(r0):
        hb = (_rms(x_ref[0, r0:r0 + sub, :], n1w_ref[...]) * (1.0 + scale) + shift).astype(BF16)
        if kv_only:
            return {n: jnp.dot(hb, win_ref[:, cols[n][0]:cols[n][1]], preferred_element_type=F32)
                    for n in ("ckv", "kr")}
        p = jnp.dot(hb, win_ref[...], preferred_element_type=F32)
        return {n: p[:, lo:hi] for n, (lo, hi) in cols.items()}

    def finish(r0, col):
        rows = slice(r0, r0 + sub)
        rope_c, rope_s = rc_ref[rows, :], rs_ref[rows, :]

        def rope(t):
            return t * rope_c + _rope_partner(t) * rope_s

        ckv = _rms(col["ckv"], kvanw_ref[...]).astype(BF16)
        kv = jnp.dot(ckv, wukv_ref[...], preferred_element_type=F32)
        k_r = col["kr"]
        ss_r = jnp.sum(k_r * k_r, axis=-1, keepdims=True)
        knw0, knw1 = knw_ref[:, :LANES], knw_ref[:, LANES:]
        kr_rot = rope(k_r * knw1)
        v_off = N_HEADS * QK_NOPE
        for h in range(N_HEADS):
            kn = kv[:, h * QK_NOPE:(h + 1) * QK_NOPE]
            r = lax.rsqrt((jnp.sum(kn * kn, axis=-1, keepdims=True) + ss_r) * (1.0 / QK_HEAD) + EPS)
            k_ref[0, h, rows, :LANES] = (kn * r * knw0).astype(BF16)
            k_ref[0, h, rows, LANES:] = (kr_rot * r).astype(BF16)
            v_ref[0, h, rows, :] = kv[:, v_off + h * V_HEAD:v_off + (h + 1) * V_HEAD].astype(BF16)
        if kv_only:
            return

        cq = _rms(col["cq"], qanw_ref[...]).astype(BF16)
        qraw = jnp.dot(cq, wuq_ref[...], preferred_element_type=F32)
        qnw0, qnw1 = qnw_ref[:, :LANES], qnw_ref[:, LANES:]
        for h in range(N_HEADS):
            q0 = qraw[:, h * HEAD_SLOT:h * HEAD_SLOT + LANES]
            q1 = qraw[:, h * HEAD_SLOT + LANES:(h + 1) * HEAD_SLOT]
            ss = jnp.sum(q0 * q0, axis=-1, keepdims=True) + jnp.sum(q1 * q1, axis=-1, keepdims=True)
            r = lax.rsqrt(ss * (1.0 / QK_HEAD) + EPS) * Q_SCALE
            q_ref[0, h, rows, :LANES] = (q0 * r * qnw0).astype(BF16)
            q_ref[0, h, rows, LANES:] = rope(q1 * r * qnw1).astype(BF16)

        gb_ref[0, rows, :] = col["gb"].astype(BF16)
        g_ref[0, rows, :] = (col["gc"] * col["u"]).astype(BF16)
        f = col["f"].astype(BF16)
        for g in range(FOURIER_GROUPS):
            res = jnp.dot(f[:, g * LANES:(g + 1) * LANES], wdft_ref[...], preferred_element_type=F32)
            fab_ref[0, 0, rows, g * LANES:(g + 1) * LANES] = res[:, :LANES].astype(BF16)
            fab_ref[0, 1, rows, g * LANES:(g + 1) * LANES] = res[:, LANES:].astype(BF16)

    starts = list(range(0, x_ref.shape[1], sub))
    pending = project(starts[0])
    for i, r0 in enumerate(starts):
        upcoming = project(starts[i + 1]) if i + 1 < len(starts) else None
        finish(r0, pending)
        pending = upcoming


def _inproj(x, mods, lw, rope_c, rope_s, wdft, *, kv_only):
    nb, s, d = x.shape
    tm = min(s, 512)
    cols = lw["cols"]
    fw = wdft.shape[0] * FOURIER_GROUPS
    cw = cols["gb"][1] - cols["gb"][0]
    kernel = functools.partial(_inproj_kernel, cols=cols, kv_only=kv_only, sub=min(tm, 256))
    head_spec = lambda w: pl.BlockSpec((1, N_HEADS, tm, w), lambda b, i: (b, 0, i, 0))
    tok_spec = lambda w: pl.BlockSpec((1, tm, w), lambda b, i: (b, i, 0))
    out_shape = [jax.ShapeDtypeStruct((nb, N_HEADS, s, HEAD_SLOT), BF16),
                 jax.ShapeDtypeStruct((nb, N_HEADS, s, V_HEAD), BF16)]
    out_specs = [head_spec(HEAD_SLOT), head_spec(V_HEAD)]
    if not kv_only:
        out_shape = ([jax.ShapeDtypeStruct((nb, N_HEADS, s, HEAD_SLOT), BF16)] + out_shape +
                     [jax.ShapeDtypeStruct((nb, s, cw), BF16), jax.ShapeDtypeStruct((nb, s, cw), BF16),
                      jax.ShapeDtypeStruct((nb, 2, s, fw), BF16)])
        out_specs = ([head_spec(HEAD_SLOT)] + out_specs +
                     [tok_spec(cw), tok_spec(cw),
                      pl.BlockSpec((1, 2, tm, fw), lambda b, i: (b, 0, i, 0))])
    weights = [lw["n1w"], lw["w_in"], lw["qanw"], lw["w_uq"], lw["kvanw"], lw["w_ukv"], lw["qnw"], lw["knw"]]
    in_specs = ([pl.BlockSpec((1, tm, d), lambda b, i: (b, i, 0)),
                 pl.BlockSpec((1, 6, d), lambda b, i: (b, 0, 0))] +
                [_const_spec(w.shape) for w in weights] +
                [pl.BlockSpec((tm, LANES), lambda b, i: (i, 0)),
                 pl.BlockSpec((tm, LANES), lambda b, i: (i, 0)),
                 _const_spec(wdft.shape)])
    return pl.pallas_call(
        kernel, grid=(nb, s // tm), in_specs=in_specs, out_specs=out_specs, out_shape=out_shape,
        compiler_params=_cparams(2), name="inproj_kv" if kv_only else "inproj",
    )(x, mods, lw["n1w"], lw["w_in"], lw["qanw"], lw["w_uq"], lw["kvanw"], lw["w_ukv"], lw["qnw"],
      lw["knw"], rope_c, rope_s, wdft)


_NT = (((1,), (1,)), ((), ()))


def _attn_kernel(q_ref, kc_ref, vc_ref, *rest, has_seq, sub):
    o_ref = rest[-1]
    key_refs = [kc_ref] + ([rest[0]] if has_seq else [])
    val_refs = [vc_ref] + ([rest[1]] if has_seq else [])

    def scores(r0):
        q = q_ref[0, 0, r0:r0 + sub, :]
        return [lax.dot_general(q, k[0, 0], _NT, preferred_element_type=F32) for k in key_refs]

    def finish(r0, s_parts):
        m = functools.reduce(jnp.maximum, [jnp.max(s, axis=-1, keepdims=True) for s in s_parts])
        p_parts = [jnp.exp2(s - m) for s in s_parts]
        l = sum(jnp.sum(p, axis=-1, keepdims=True) for p in p_parts)
        o = sum(jnp.dot(p.astype(BF16), v[0, 0], preferred_element_type=F32) for p, v in zip(p_parts, val_refs))
        o_ref[0, r0:r0 + sub, :] = (o / l).astype(BF16)

    starts = list(range(0, q_ref.shape[2], sub))
    pending = scores(starts[0])
    for nxt in starts[1:] + [None]:
        upcoming = scores(nxt) if nxt is not None else None
        finish(nxt - sub if nxt is not None else starts[-1], pending)
        pending = upcoming


def _attention(q, kc, vc, k=None, v=None):
    nb, nh, s, _ = q.shape
    n_ctx = kc.shape[2]
    has_seq = k is not None
    tq = min(s, 2048)
    kv_spec = lambda n, w: pl.BlockSpec((1, 1, n, w), lambda b, h, i: (b, h, 0, 0))
    in_specs = [pl.BlockSpec((1, 1, tq, HEAD_SLOT), lambda b, h, i: (b, h, i, 0)),
                kv_spec(n_ctx, HEAD_SLOT), kv_spec(n_ctx, V_HEAD)]
    args = [q, kc, vc]
    if has_seq:
        in_specs += [kv_spec(s, HEAD_SLOT), kv_spec(s, V_HEAD)]
        args += [k, v]
    return pl.pallas_call(
        functools.partial(_attn_kernel, has_seq=has_seq, sub=min(tq, 256)),
        grid=(nb, nh, s // tq), in_specs=in_specs,
        out_specs=pl.BlockSpec((1, tq, V_HEAD), lambda b, h, i: (b, i, h)),
        out_shape=jax.ShapeDtypeStruct((nb, s, nh * V_HEAD), BF16),
        compiler_params=_cparams(3), name="attention" if has_seq else "attention_ctx",
    )(*args)


def _seqdft_kernel(c_ref, s_ref, fab_ref, o_ref, *, scale):
    y = jnp.dot(c_ref[...], fab_ref[0, 0], preferred_element_type=F32)
    y = y + jnp.dot(s_ref[...], fab_ref[0, 1], preferred_element_type=F32)
    o_ref[0] = (y * scale).astype(BF16)


def _seqdft(dft_c, dft_s, fab, group_w):
    nb, _, s, fw = fab.shape
    tr = min(s, 1024)
    scale = 1.0 / math.sqrt(s * group_w)
    return pl.pallas_call(
        functools.partial(_seqdft_kernel, scale=scale),
        grid=(s // tr, nb),
        in_specs=[pl.BlockSpec((tr, s), lambda i, b: (i, 0)),
                  pl.BlockSpec((tr, s), lambda i, b: (i, 0)),
                  pl.BlockSpec((1, 2, s, fw), lambda i, b: (b, 0, 0, 0))],
        out_specs=pl.BlockSpec((1, tr, fw), lambda i, b: (b, i, 0)),
        out_shape=jax.ShapeDtypeStruct((nb, s, fw), BF16),
        compiler_params=_cparams(2), name="seqdft",
    )(dft_c, dft_s, fab)


HALO = 16


def _mix_kernel(attn_ref, gb_ref, g_ref, gprev_ref, gnext_ref, fy_ref, x_ref, mod_ref, convw_ref,
                wout_ref, n2w_ref, wrcat_ref, wrhi_ref, br_ref, cntin_ref,
                x1_ref, h2_ref, route_ref, routet_ref, cnt_ref, carry_ref, *, tm, sub, n_tiles, attn_w, conv_w):
    b, i = pl.program_id(0), pl.program_id(1)

    @pl.when((b == 0) & (i == 0))
    def _():
        carry_ref[...] = cntin_ref[...]

    def mix_rows(r0):
        rows = slice(r0, r0 + sub)
        g = g_ref[0, rows, :].astype(F32)
        row = lax.broadcasted_iota(jnp.int32, g.shape, 0)
        if r0 == 0:
            prev_row = jnp.where(i > 0, gprev_ref[0, HALO - 1:HALO, :].astype(F32), 0.0)
        else:
            prev_row = g_ref[0, r0 - 1:r0, :].astype(F32)
        if r0 + sub == tm:
            next_row = jnp.where(i < n_tiles - 1, gnext_ref[0, 0:1, :].astype(F32), 0.0)
        else:
            next_row = g_ref[0, r0 + sub:r0 + sub + 1, :].astype(F32)
        g_dn = jnp.where(row == 0, prev_row, pltpu.roll(g, 1, axis=0))
        g_up = jnp.where(row == sub - 1, next_row, pltpu.roll(g, sub - 1, axis=0))
        conv = gb_ref[0, rows, :].astype(F32) * (
            g_dn * convw_ref[0:1, :] + g * convw_ref[1:2, :] + g_up * convw_ref[2:3, :])

        mix = jnp.dot(attn_ref[0, rows, :], wout_ref[:attn_w, :], preferred_element_type=F32)
        mix = mix + jnp.dot(conv.astype(BF16), wout_ref[attn_w:attn_w + conv_w, :], preferred_element_type=F32)
        mix = mix + jnp.dot(fy_ref[0, rows, :], wout_ref[attn_w + conv_w:, :], preferred_element_type=F32)
        x1 = x_ref[0, rows, :] + mod_ref[0, 2:3, :] * mix
        x1_ref[0, rows, :] = x1
        h2 = _rms(x1, n2w_ref[...]) * (1.0 + mod_ref[0, 4:5, :]) + mod_ref[0, 3:4, :]
        _pack_rows(h2_ref, h2, r0)
        return h2

    def route_rows(r0, h2):
        rows = slice(r0, r0 + sub)
        hi = h2.astype(BF16)
        lo = (h2 - hi.astype(F32)).astype(BF16)
        a = jnp.dot(hi, wrcat_ref[...], preferred_element_type=F32)
        logits = a[:, :LANES] + a[:, LANES:] + jnp.dot(lo, wrhi_ref[...], preferred_element_type=F32) + br_ref[...]

        lane = lax.broadcasted_iota(jnp.int32, logits.shape, 1)
        lane_f = lane.astype(F32)
        no_lane = float(LANES)
        is_group = lane < N_GROUPS
        gl = jnp.where(is_group, logits, NEG_BIG)
        gmax = jnp.max(gl, axis=-1, keepdims=True)
        gidx = jnp.min(jnp.where(gl == gmax, lane_f, no_lane), axis=-1, keepdims=True)
        g_p = 1.0 / jnp.sum(jnp.where(is_group, jnp.exp(gl - gmax), 0.0), axis=-1, keepdims=True)
        lane_group = ((lane - N_GROUPS) >> 3).astype(F32)
        in_group = (lane >= N_GROUPS) & (lane < N_GROUPS + N_EXPERTS) & (lane_group == gidx)
        el = jnp.where(in_group, logits, NEG_BIG)
        m1 = jnp.max(el, axis=-1, keepdims=True)
        i1 = jnp.min(jnp.where(el == m1, lane_f, no_lane), axis=-1, keepdims=True)
        el2 = jnp.where(lane_f == i1, NEG_BIG, el)
        m2 = jnp.max(el2, axis=-1, keepdims=True)
        i2 = jnp.min(jnp.where(el2 == m2, lane_f, no_lane), axis=-1, keepdims=True)
        e2 = jnp.exp(m2 - m1)
        w1 = g_p / (1.0 + e2)
        w2 = g_p * e2 / (1.0 + e2)

        hit1, hit2 = lane_f == i1, lane_f == i2
        onehot = jnp.where(hit1 | hit2, 1.0, 0.0)
        ti = lax.broadcasted_iota(jnp.int32, (sub, sub), 0)
        tj = lax.broadcasted_iota(jnp.int32, (sub, sub), 1)
        before = jnp.where(tj < ti, 1.0, 0.0).astype(BF16)
        seen = jnp.dot(before, onehot.astype(BF16), preferred_element_type=F32) + carry_ref[...]
        r1 = jnp.sum(jnp.where(hit1, seen, 0.0), axis=-1, keepdims=True)
        r2 = jnp.sum(jnp.where(hit2, seen, 0.0), axis=-1, keepdims=True)
        carry_ref[...] = carry_ref[...] + jnp.sum(onehot, axis=0, keepdims=True)

        route = jnp.where(lane == 0, i1 - N_GROUPS, 0.0)
        route = jnp.where(lane == 1, i2 - N_GROUPS, route)
        route = jnp.where(lane == 2, r1, route)
        route = jnp.where(lane == 3, r2, route)
        route = jnp.where(lane == 4, w1, route)
        route = jnp.where(lane == 5, w2, route)
        route_ref[rows, :] = route
        routet_ref[:, rows] = route.T[:ROUTE_ROWS, :]

    starts = list(range(0, tm, sub))
    pending = mix_rows(starts[0])
    for n, r0 in enumerate(starts):
        upcoming = mix_rows(starts[n + 1]) if n + 1 < len(starts) else None
        route_rows(r0, pending)
        pending = upcoming
    cnt_ref[...] = carry_ref[...]


def _mix(attn, gb, g, fy, x, mods, lw, cnt_in):
    nb, s, d = x.shape
    tm = min(s, 512)
    n_tiles = s // tm
    attn_w, conv_w = attn.shape[-1], gb.shape[-1]
    tok = lambda w: pl.BlockSpec((1, tm, w), lambda b, i: (b, i, 0))
    hb = tm // HALO
    in_specs = [tok(attn_w), tok(conv_w), tok(conv_w),
                pl.BlockSpec((1, HALO, conv_w), lambda b, i: (b, jnp.maximum(i * hb - 1, 0), 0)),
                pl.BlockSpec((1, HALO, conv_w), lambda b, i: (b, jnp.minimum((i + 1) * hb, s // HALO - 1), 0)),
                tok(fy.shape[-1]), tok(d),
                pl.BlockSpec((1, 6, d), lambda b, i: (b, 0, 0))]
    weights = [lw["conv_w"], lw["w_out"], lw["n2w"], lw["wr_cat"], lw["wr_hi"], lw["b_r"], cnt_in]
    in_specs += [_const_spec(w.shape) for w in weights]
    n_tok = nb * s
    out_shape = [jax.ShapeDtypeStruct((nb, s, d), F32),
                 jax.ShapeDtypeStruct((n_tok * WORD_ROWS, LANES), U32),
                 jax.ShapeDtypeStruct((n_tok, LANES), F32),
                 jax.ShapeDtypeStruct((ROUTE_ROWS, n_tok), F32),
                 jax.ShapeDtypeStruct((1, LANES), F32)]
    out_specs = [tok(d),
                 pl.BlockSpec((tm * WORD_ROWS, LANES), lambda b, i: (b * n_tiles + i, 0)),
                 pl.BlockSpec((tm, LANES), lambda b, i: (b * n_tiles + i, 0)),
                 pl.BlockSpec((ROUTE_ROWS, tm), lambda b, i: (0, b * n_tiles + i)),
                 pl.BlockSpec((1, LANES), lambda b, i: (0, 0))]
    return pl.pallas_call(
        functools.partial(_mix_kernel, tm=tm, sub=min(tm, 256), n_tiles=n_tiles, attn_w=attn_w, conv_w=conv_w),
        grid=(nb, n_tiles), in_specs=in_specs, out_specs=out_specs, out_shape=out_shape,
        scratch_shapes=[pltpu.VMEM((1, LANES), F32)],
        compiler_params=_cparams(2), name="mix_router",
    )(attn, gb, g, g, g, fy, x, mods, *weights)


def _row(ref, idx):
    return ref.at[pl.ds(pl.multiple_of(idx * WORD_ROWS, WORD_ROWS), WORD_ROWS), :]


DMA_UNROLL = 8
ZERO_ROWS = 128


def _row_dma_loop(n_rows, copies, method):
    def body(jo, carry):
        for u in range(DMA_UNROLL):
            for c in copies(jo * DMA_UNROLL + u):
                getattr(c, method)()
        return carry
    lax.fori_loop(0, n_rows // DMA_UNROLL, body, 0)


def _tile_slots(slots_kt, tm):
    n_tok = slots_kt.shape[1]
    return slots_kt.reshape(TOP_K, n_tok // tm, tm).transpose(1, 0, 2).reshape(n_tok // tm, 1, TOP_K * tm)


def _dispatch_kernel(lo_ref, n_ref, slot_ref, *refs, tm, n_main_tiles, has_ctx):
    if has_ctx:
        h2_ref, h2c_ref, xs_ref, zero_ref, sem, zsem = refs
    else:
        h2_ref, xs_ref, zero_ref, sem, zsem = refs
    i = pl.program_id(0)

    @pl.when(i == 0)
    def _():
        zero_ref[...] = jnp.zeros(zero_ref.shape, U32)

        def zero_copy(first, rows):
            return pltpu.make_async_copy(
                zero_ref.at[pl.ds(0, rows * WORD_ROWS), :],
                xs_ref.at[pl.ds(pl.multiple_of(first * WORD_ROWS, WORD_ROWS), rows * WORD_ROWS), :], zsem)

        def fill(e, carry):
            lo, n = lo_ref[e], n_ref[e]
            n_full = n // ZERO_ROWS

            def run(method):
                def full(c, cc):
                    getattr(zero_copy(lo + c * ZERO_ROWS, ZERO_ROWS), method)()
                    return cc
                lax.fori_loop(0, n_full, full, 0)
                first = lo + n_full * ZERO_ROWS
                p = ZERO_ROWS // 2
                while p >= 1:
                    @pl.when((n & p) != 0)
                    def _(first=first, p=p):
                        getattr(zero_copy(first, p), method)()
                    first = first + (n & p)
                    p //= 2
            run("start")
            run("wait")
            return carry

        lax.fori_loop(0, lo_ref.shape[0], fill, 0)

    def scatter(src_ref):
        copies = lambda j: [
            pltpu.make_async_copy(_row(src_ref, j), _row(xs_ref, slot_ref[0, 0, k * tm + j]), sem)
            for k in range(TOP_K)]
        _row_dma_loop(tm, copies, "start")
        _row_dma_loop(tm, copies, "wait")

    if has_ctx:
        @pl.when(i < n_main_tiles)
        def _():
            scatter(h2_ref)

        @pl.when(i >= n_main_tiles)
        def _():
            scatter(h2c_ref)
    else:
        scatter(h2_ref)


def _dispatch(slots_kt, fill_lo, fill_n, h2, h2c, n_slots):
    tm = MOE_BLOCK
    n_main_tiles = h2.shape[0] // (tm * WORD_ROWS)
    has_ctx = h2c is not None
    slots3 = _tile_slots(slots_kt, tm)
    n_tiles = slots3.shape[0]
    blk = (tm * WORD_ROWS, LANES)
    in_specs = [pl.BlockSpec((1, 1, tm * TOP_K), lambda i, lo, n: (i, 0, 0), memory_space=pltpu.SMEM),
                pl.BlockSpec(blk, lambda i, lo, n: (jnp.minimum(i, n_main_tiles - 1), 0))]
    args = [slots3, h2]
    if has_ctx:
        in_specs.append(pl.BlockSpec(blk, lambda i, lo, n: (jnp.maximum(i - n_main_tiles, 0), 0)))
        args.append(h2c)
    grid_spec = pltpu.PrefetchScalarGridSpec(
        num_scalar_prefetch=2, grid=(n_tiles,), in_specs=in_specs,
        out_specs=pl.BlockSpec(memory_space=pl.ANY),
        scratch_shapes=[pltpu.VMEM((ZERO_ROWS * WORD_ROWS, LANES), U32),
                        pltpu.SemaphoreType.DMA(()), pltpu.SemaphoreType.DMA(())])
    return pl.pallas_call(
        functools.partial(_dispatch_kernel, tm=tm, n_main_tiles=n_main_tiles, has_ctx=has_ctx),
        grid_spec=grid_spec,
        out_shape=jax.ShapeDtypeStruct((n_slots * WORD_ROWS, LANES), U32),
        compiler_params=_cparams(1), name="dispatch",
    )(fill_lo, fill_n, *args)


def _expert_kernel(bexp_ref, nused_ref, xs_ref, wg_ref, wu_ref, wd_ref, y_ref,
                   xb0_ref, xb1_ref, yb0_ref, yb1_ref, wgu_ref, wdb_ref, *, d_expert, n_blocks):
    s = pl.program_id(0)
    n_used = nused_ref[0]
    half = xb0_ref.shape[1] // 2

    def unpack(xb_ref):
        for r in range(WORD_ROWS):
            lo, hi = _unpack_pair(xs_ref[pl.ds(r, MOE_BLOCK, stride=WORD_ROWS), :])
            xb_ref[:, r * LANES:(r + 1) * LANES] = lo.astype(BF16)
            xb_ref[:, half + r * LANES:half + (r + 1) * LANES] = hi.astype(BF16)

    def mlp(xb_ref, yb_ref):
        gu = jnp.dot(xb_ref[...], wgu_ref[...], preferred_element_type=F32)
        gate, up = gu[:, :d_expert], gu[:, d_expert:]
        hid = (gate * jax.nn.sigmoid(gate) * up).astype(BF16)
        yb_ref[...] = jnp.dot(hid, wdb_ref[...], preferred_element_type=F32)

    def store(yb_ref):
        _pack_rows(y_ref, yb_ref[...])

    run_unpack = s < n_used
    run_mlp = (s >= 1) & (s <= n_used)
    run_store = (s >= 2) & (s <= n_used + 1)
    steady = (s >= 2) & (s < n_used)

    mlp_block = jnp.clip(s - 1, 0, n_blocks - 1)
    new_expert = (s == 1) | (bexp_ref[mlp_block] != bexp_ref[jnp.maximum(mlp_block - 1, 0)])

    @pl.when(run_mlp & new_expert)
    def _():
        wgu_ref[:, :d_expert] = wg_ref[0, 0].astype(BF16)
        wgu_ref[:, d_expert:] = wu_ref[0, 0].astype(BF16)
        wdb_ref[...] = wd_ref[0, 0].astype(BF16)

    for parity, (xb_new, xb_cur, yb_cur, yb_old) in enumerate(
            [(xb0_ref, xb1_ref, yb1_ref, yb0_ref), (xb1_ref, xb0_ref, yb0_ref, yb1_ref)]):
        mine = (s % 2) == parity

        @pl.when(steady & mine)
        def _():
            unpack(xb_new)
            mlp(xb_cur, yb_cur)
            store(yb_old)

        @pl.when(jnp.logical_not(steady) & mine)
        def _():
            @pl.when(run_unpack)
            def _():
                unpack(xb_new)

            @pl.when(run_mlp)
            def _():
                mlp(xb_cur, yb_cur)

            @pl.when(run_store)
            def _():
                store(yb_old)

    @pl.when(s >= n_used + 2)
    def _():
        y_ref[...] = jnp.zeros(y_ref.shape, U32)


def _experts(layer, block_exp, n_used, xs, w_gate, w_up, w_down):
    _, _, d, d_expert = w_gate.shape
    n_blocks = block_exp.shape[0]
    blk = (MOE_BLOCK * WORD_ROWS, LANES)
    w_idx = lambda s, be, nu: (layer, be[jnp.clip(s - 1, 0, n_blocks - 1)], 0, 0)
    grid_spec = pltpu.PrefetchScalarGridSpec(
        num_scalar_prefetch=2, grid=(n_blocks + 2,),
        in_specs=[pl.BlockSpec(blk, lambda s, be, nu: (jnp.minimum(s, nu[0] - 1), 0)),
                  pl.BlockSpec((1, 1, d, d_expert), w_idx),
                  pl.BlockSpec((1, 1, d, d_expert), w_idx),
                  pl.BlockSpec((1, 1, d_expert, d), w_idx)],
        out_specs=pl.BlockSpec(blk, lambda s, be, nu: (jnp.maximum(s - 2, 0), 0)),
        scratch_shapes=[pltpu.VMEM((MOE_BLOCK, d), BF16), pltpu.VMEM((MOE_BLOCK, d), BF16),
                        pltpu.VMEM((MOE_BLOCK, d), F32), pltpu.VMEM((MOE_BLOCK, d), F32),
                        pltpu.VMEM((d, 2 * d_expert), BF16),
                        pltpu.VMEM((d_expert, d), BF16)])
    return pl.pallas_call(
        functools.partial(_expert_kernel, d_expert=d_expert, n_blocks=n_blocks),
        grid_spec=grid_spec,
        out_shape=jax.ShapeDtypeStruct(xs.shape, U32),
        compiler_params=_cparams(1), name="experts",
    )(block_exp, n_used, xs, w_gate, w_up, w_down)


def _combine_kernel(slot_ref, slotn_ref, x1_ref, mod_ref, route_ref, y_ref, o_ref, ybuf_ref, sem, *, tm, n_total):
    t = pl.program_id(0)
    half_rows = TOP_K * tm
    cur = t % 2

    def gathers(slots, buf):
        return lambda j: [
            pltpu.make_async_copy(_row(y_ref, slots[0, 0, k * tm + j]), _row(ybuf_ref, buf * half_rows + k * tm + j),
                                  sem.at[buf])
            for k in range(TOP_K)]

    @pl.when(t == 0)
    def _():
        _row_dma_loop(tm, gathers(slot_ref, 0), "start")

    @pl.when(t + 1 < n_total)
    def _():
        _row_dma_loop(tm, gathers(slotn_ref, 1 - cur), "start")

    _row_dma_loop(tm, gathers(slot_ref, cur), "wait")

    w1, w2 = route_ref[:, 4:5], route_ref[:, 5:6]
    base = cur * half_rows * WORD_ROWS
    half = x1_ref.shape[2] // 2
    for r in range(WORD_ROWS):
        a_lo, a_hi = _unpack_pair(ybuf_ref[pl.ds(base + r, tm, stride=WORD_ROWS), :])
        b_lo, b_hi = _unpack_pair(ybuf_ref[pl.ds(base + tm * WORD_ROWS + r, tm, stride=WORD_ROWS), :])
        for off, ya, yb in ((0, a_lo, b_lo), (half, a_hi, b_hi)):
            lanes = slice(off + r * LANES, off + (r + 1) * LANES)
            o_ref[0, :, lanes] = x1_ref[0, :, lanes] + mod_ref[0, 5:6, lanes] * (ya * w1 + yb * w2)


def _combine(slots_kt, x1, mods, route, y):
    nb, s, d = x1.shape
    tm = min(s, MOE_BLOCK)
    n_tiles = s // tm
    n_total = nb * n_tiles
    slots3 = _tile_slots(slots_kt, tm)
    slot_spec = lambda nxt: pl.BlockSpec((1, 1, tm * TOP_K), lambda t: (jnp.minimum(t + nxt, n_total - 1), 0, 0),
                                         memory_space=pltpu.SMEM)
    return pl.pallas_call(
        functools.partial(_combine_kernel, tm=tm, n_total=n_total),
        grid=(n_total,),
        in_specs=[slot_spec(0), slot_spec(1),
                  pl.BlockSpec((1, tm, d), lambda t: (t // n_tiles, t % n_tiles, 0)),
                  pl.BlockSpec((1, 6, d), lambda t: (t // n_tiles, 0, 0)),
                  pl.BlockSpec((tm, LANES), lambda t: (t, 0)),
                  pl.BlockSpec(memory_space=pl.ANY)],
        out_specs=pl.BlockSpec((1, tm, d), lambda t: (t // n_tiles, t % n_tiles, 0)),
        out_shape=jax.ShapeDtypeStruct((nb, s, d), F32),
        scratch_shapes=[pltpu.VMEM((2 * TOP_K * tm * WORD_ROWS, LANES), U32), pltpu.SemaphoreType.DMA((2,))],
        compiler_params=_cparams(1), name="combine",
    )(slots3, slots3, x1, mods, route, y)


def _layer_weights(l, d, norm1_w, norm2_w, w_in, q_a_norm_w, w_uq, kv_a_norm_w, w_ukv, q_norm_w, k_norm_w,
                   conv_w, w_out, w_rg, b_rg, w_re, b_re):
    q_lora, kv_lora = q_a_norm_w.shape[1], kv_a_norm_w.shape[1]
    conv_cols = conv_w.shape[2]
    in_cols = w_in.shape[2]
    four_cols = in_cols - (q_lora + kv_lora + QK_ROPE + 3 * conv_cols)
    o_ckv = q_lora
    o_kr = o_ckv + kv_lora
    o_gb = o_kr + QK_ROPE
    o_gc, o_u, o_f = o_gb + conv_cols, o_gb + 2 * conv_cols, o_gb + 3 * conv_cols
    wi = w_in[l]
    w_in_p = jnp.concatenate([wi[:, :o_kr], wi[:, o_gb:], wi[:, o_kr:o_gb],
                              jnp.zeros((d, LANES - QK_ROPE), F32)], axis=1).astype(BF16)
    names, widths = ["cq", "ckv", "gb", "gc", "u", "f", "kr"], [q_lora, kv_lora, conv_cols, conv_cols, conv_cols, four_cols, LANES]
    cols, o = {}, 0
    for n, w in zip(names, widths):
        cols[n] = (o, o + w)
        o += w
    pad_head = lambda w: jnp.pad(w, [(0, 0)] * (w.ndim - 1) + [(0, HEAD_SLOT - QK_HEAD)])
    w_uq_p = pad_head(w_uq[l].reshape(q_lora, N_HEADS, QK_HEAD)).reshape(q_lora, N_HEADS * HEAD_SLOT).astype(BF16)
    w_ukv_p = (w_ukv[l].reshape(kv_lora, N_HEADS, 2, QK_NOPE).transpose(0, 2, 1, 3)
               .reshape(kv_lora, 2 * N_HEADS * QK_NOPE).astype(BF16))
    w_r = jnp.concatenate([w_rg[l], w_re[l], jnp.zeros((d, LANES - N_GROUPS - N_EXPERTS), F32)], axis=1)
    wr_hi = w_r.astype(BF16)
    wr_lo = (w_r - wr_hi.astype(F32)).astype(BF16)
    b_r = jnp.concatenate([b_rg[l], b_re[l], jnp.zeros((LANES - N_GROUPS - N_EXPERTS,), F32)]).reshape(1, LANES)
    return dict(cols=cols, n1w=norm1_w[l].reshape(1, d), n2w=norm2_w[l].reshape(1, d), w_in=w_in_p,
                qanw=q_a_norm_w[l].reshape(1, q_lora), w_uq=w_uq_p, kvanw=kv_a_norm_w[l].reshape(1, kv_lora),
                w_ukv=w_ukv_p, qnw=pad_head(q_norm_w[l]).reshape(1, HEAD_SLOT),
                knw=pad_head(k_norm_w[l]).reshape(1, HEAD_SLOT), conv_w=conv_w[l], w_out=w_out[l].astype(BF16),
                wr_cat=jnp.concatenate([wr_hi, wr_lo], axis=1), wr_hi=wr_hi, b_r=b_r)


def _rope_tables(n_tok):
    freqs = QK_ROPE // 4
    pos = jnp.arange(n_tok)
    row = (pos // GRID_W).astype(F32)
    colp = (pos % GRID_W).astype(F32)
    inv = ROPE_THETA ** (-jnp.arange(freqs, dtype=F32) / freqs)
    ar, ac = row[:, None] * inv, colp[:, None] * inv
    zeros = jnp.zeros((n_tok, LANES - QK_ROPE), F32)
    rope_c = jnp.concatenate([jnp.cos(ar), jnp.cos(ar), jnp.cos(ac), jnp.cos(ac), zeros], axis=1)
    rope_s = jnp.concatenate([-jnp.sin(ar), jnp.sin(ar), -jnp.sin(ac), jnp.sin(ac), zeros], axis=1)
    return rope_c, rope_s


def _identity_rope(n_tok):
    ones = jnp.concatenate([jnp.ones((n_tok, QK_ROPE), F32), jnp.zeros((n_tok, LANES - QK_ROPE), F32)], axis=1)
    return ones, jnp.zeros((n_tok, LANES), F32)


def _dft_tables(n):
    idx = jnp.arange(n, dtype=jnp.int32)
    ang = ((idx[:, None] * idx[None, :]) % n).astype(F32) * (2.0 * math.pi / n)
    return jnp.cos(ang), jnp.sin(ang)


def _slot_plan(route_t, counts):
    n_tok = route_t.shape[1]
    expert = route_t[0:TOP_K].astype(jnp.int32)
    rank = route_t[TOP_K:2 * TOP_K].astype(jnp.int32)
    padded = (counts + MOE_BLOCK - 1) // MOE_BLOCK * MOE_BLOCK
    pad_ends = jnp.cumsum(padded)
    pad_starts = pad_ends - padded
    slots = pad_starts[expert] + rank
    n_blocks = -(-(n_tok * TOP_K) // MOE_BLOCK) + N_EXPERTS
    n_slots = n_blocks * MOE_BLOCK
    block_start = jnp.arange(n_blocks, dtype=jnp.int32) * MOE_BLOCK
    block_exp = jnp.sum((pad_ends[None, :] <= block_start[:, None]).astype(jnp.int32), axis=1)
    block_exp = jnp.minimum(block_exp, N_EXPERTS - 1)
    n_used = (pad_ends[-1] // MOE_BLOCK).astype(jnp.int32)
    last_exp = block_exp[jnp.maximum(n_used - 1, 0)]
    block_exp = jnp.where(jnp.arange(n_blocks) < n_used, block_exp, last_exp)
    fill_lo = jnp.concatenate([pad_starts + counts, pad_ends[-1:]]).astype(jnp.int32)
    fill_n = jnp.concatenate([padded - counts, n_slots - pad_ends[-1:]]).astype(jnp.int32)
    return slots.astype(jnp.int32), block_exp, n_used.reshape(1), fill_lo, fill_n, n_slots


def kernel(x, c, ctx, c_ctx, w_mod, b_mod, norm1_w, norm2_w, w_in, q_a_norm_w, w_uq, kv_a_norm_w, w_ukv, q_norm_w, k_norm_w, conv_w, w_out, w_router_group, b_router_group, w_router_expert, b_router_expert, w_gate, w_up, w_down):
    nb, s, d = x.shape
    n_ctx = ctx.shape[1]
    depth = w_mod.shape[0]
    group_w = conv_w.shape[2] // FOURIER_GROUPS

    rows = -(-(nb + 1) // 8) * 8
    c_all = jnp.concatenate([c, c_ctx[None, :], jnp.zeros((rows - nb - 1, d), F32)], axis=0)
    mod_all = _modulation(c_all, w_mod, b_mod)

    rope_c, rope_s = _rope_tables(s)
    id_c, id_s = _identity_rope(n_ctx)
    cc, sc = _dft_tables(group_w)
    wdft = jnp.concatenate([cc, -sc], axis=1).astype(BF16)
    seq_c, seq_s = (t.astype(BF16) for t in _dft_tables(s))
    ctx_c, ctx_s = (t.astype(BF16) for t in _dft_tables(n_ctx))

    xc = ctx
    for l in range(depth):
        last = l == depth - 1
        lw = _layer_weights(l, d, norm1_w, norm2_w, w_in, q_a_norm_w, w_uq, kv_a_norm_w, w_ukv, q_norm_w,
                            k_norm_w, conv_w, w_out, w_router_group, b_router_group, w_router_expert,
                            b_router_expert)
        mods = mod_all[l, :nb].reshape(nb, 6, d)
        mods_c = jnp.broadcast_to(mod_all[l, nb].reshape(1, 6, d), (nb, 6, d))

        q, k, v, gb, g, fab = _inproj(x, mods, lw, rope_c, rope_s, wdft, kv_only=False)
        if last:
            kc, vc = _inproj(xc, mods_c, lw, id_c, id_s, wdft, kv_only=True)
        else:
            qc, kc, vc, gbc, gc, fabc = _inproj(xc, mods_c, lw, id_c, id_s, wdft, kv_only=False)
        attn = _attention(q, kc, vc, k, v)
        fy = _seqdft(seq_c, seq_s, fab, group_w)
        zero_cnt = jnp.zeros((1, LANES), F32)
        x1, h2, route, route_t, cnt = _mix(attn, gb, g, fy, x, mods, lw, zero_cnt)
        if not last:
            attn_c = _attention(qc, kc, vc)
            fyc = _seqdft(ctx_c, ctx_s, fabc, group_w)
            x1c, h2c, route_c, route_tc, cnt = _mix(attn_c, gbc, gc, fyc, xc, mods_c, lw, cnt)
            route_t = jnp.concatenate([route_t, route_tc], axis=1)
        else:
            h2c = None

        counts = cnt[0, N_GROUPS:N_GROUPS + N_EXPERTS].astype(jnp.int32)
        slots, block_exp, n_used, fill_lo, fill_n, n_slots = _slot_plan(route_t, counts)
        xs = _dispatch(slots, fill_lo, fill_n, h2, h2c, n_slots)
        y = _experts(l, block_exp, n_used, xs, w_gate, w_up, w_down)
        n_main = nb * s
        x = _combine(slots[:, :n_main], x1, mods, route, y)
        if not last:
            xc = _combine(slots[:, n_main:], x1c, mods_c, route_c, y)
    return x
```

```python
import functools
import math

import jax
import jax.numpy as jnp
from jax import lax
from jax.experimental import pallas as pl
from jax.experimental.pallas import tpu as pltpu

F32 = jnp.float32
BF16 = jnp.bfloat16

EPS = 1e-6
N_HEADS = 8
QK_NOPE = 128
QK_ROPE = 64
QK_HEAD = QK_NOPE + QK_ROPE
V_HEAD = 128
GRID_W = 64
ROPE_THETA = 10000.0
ATTN_SCALE = QK_HEAD ** -0.5
Q_SCALE = ATTN_SCALE * math.log2(math.e)
CONV_K = 3
FOURIER_GROUPS = 4
N_GROUPS = 8
EXPERTS_PER_GROUP = 8
N_EXPERTS = N_GROUPS * EXPERTS_PER_GROUP
TOP_K = 2
MOE_BLOCK = 256

LANES = 128
HEAD_SLOT = 2 * LANES
WORD_ROWS = 8
ROUTE_ROWS = 8
U32 = jnp.uint32
PROJ_BLOCK = 512
VMEM_LIMIT = 56 * 1024 * 1024
NEG_BIG = -1e30


def _cparams(n_axes):
    return pltpu.CompilerParams(dimension_semantics=("arbitrary",) * n_axes,
                                vmem_limit_bytes=VMEM_LIMIT)


def _pack_pair(lo, hi):
    lo_bits = lax.bitcast_convert_type(lo.astype(BF16).astype(F32), U32) >> 16
    hi_bits = lax.bitcast_convert_type(hi.astype(BF16).astype(F32), U32) & U32(0xFFFF0000)
    return hi_bits | lo_bits


def _unpack_pair(words):
    lo = lax.bitcast_convert_type(words << 16, F32)
    hi = lax.bitcast_convert_type(words & U32(0xFFFF0000), F32)
    return lo, hi


def _pack_rows(dst_ref, val):
    rows, d = val.shape
    half = d // 2
    for r in range(WORD_ROWS):
        dst_ref[pl.ds(r, rows, stride=WORD_ROWS), :] = _pack_pair(
            val[:, r * LANES:(r + 1) * LANES], val[:, half + r * LANES:half + (r + 1) * LANES])


def _const_spec(shape):
    nd = len(shape)
    return pl.BlockSpec(shape, lambda *_: (0,) * nd, pipeline_mode=pl.Buffered(1))


def _mod_kernel(c_ref, w_ref, b_ref, o_ref):
    c = c_ref[...]
    a = (c * jax.nn.sigmoid(c)).astype(BF16)
    o_ref[0] = jnp.dot(a, w_ref[0].astype(BF16), preferred_element_type=F32) + b_ref[0]


def _modulation(c_all, w_mod, b_mod):
    n_layers, d, n_out = w_mod.shape
    rows = c_all.shape[0]
    tn = 1024
    return pl.pallas_call(
        _mod_kernel,
        grid=(n_layers, n_out // tn),
        in_specs=[pl.BlockSpec((rows, d), lambda l, j: (0, 0)),
                  pl.BlockSpec((1, d, tn), lambda l, j: (l, 0, j)),
                  pl.BlockSpec((1, 1, tn), lambda l, j: (l, 0, j))],
        out_specs=pl.BlockSpec((1, rows, tn), lambda l, j: (l, 0, j)),
        out_shape=jax.ShapeDtypeStruct((n_layers, rows, n_out), F32),
        compiler_params=_cparams(2),
        name="modulation",
    )(c_all, w_mod, b_mod.reshape(n_layers, 1, n_out))


def _rms(t, w):
    return t * lax.rsqrt(jnp.mean(t * t, axis=-1, keepdims=True) + EPS) * w


def _rope_partner(t):
    lane = lax.broadcasted_iota(jnp.int32, t.shape, 1)
    ahead = pltpu.roll(t, LANES - 16, axis=1)
    behind = pltpu.roll(t, 16, axis=1)
    return jnp.where((lane & 16) == 0, ahead, behind)


def _inproj_kernel(x_ref, mod_ref, n1w_ref, win_ref, qanw_ref, wuq_ref, kvanw_ref, wukv_ref,
                   qnw_ref, knw_ref, rc_ref, rs_ref, wdft_ref, *rest, cols, pcols, kv_only, n_total):
    if kv_only:
        k_ref, v_ref, p0_ref, p1_ref = rest
    else:
        q_ref, k_ref, v_ref, gb_ref, g_ref, fab_ref, p0_ref, p1_ref = rest
    t = pl.program_id(0)

    def project(p_ref):
        state = {}

        def norm():
            shift, scale = mod_ref[0, 0:1, :], mod_ref[0, 1:2, :]
            state["hb"] = (_rms(x_ref[0], n1w_ref[...]) * (1.0 + scale) + shift).astype(BF16)

        def block(src, dst):
            def run():
                p_ref[:, dst[0]:dst[1]] = jnp.dot(state["hb"], win_ref[:, src[0]:src[1]],
                                                   preferred_element_type=F32)
            return run

        if kv_only:
            return [norm] + [block(cols[n], pcols[n]) for n in pcols]
        width = win_ref.shape[1]
        return [norm] + [block((lo, min(lo + PROJ_BLOCK, width)), (lo, min(lo + PROJ_BLOCK, width)))
                         for lo in range(0, width, PROJ_BLOCK)]

    def finish(p_ref):
        col = lambda n: p_ref[:, pcols[n][0]:pcols[n][1]]
        state = {}

        def rope(v):
            return v * rc_ref[...] + _rope_partner(v) * rs_ref[...]

        def kv_prep():
            ckv = _rms(col("ckv"), kvanw_ref[...]).astype(BF16)
            state["kv"] = jnp.dot(ckv, wukv_ref[...], preferred_element_type=F32)
            k_r = col("kr")
            state["ss_r"] = jnp.sum(k_r * k_r, axis=-1, keepdims=True)
            state["kr_rot"] = rope(k_r * knw_ref[:, LANES:])

        def k_head(h):
            def run():
                kv, v_off = state["kv"], N_HEADS * QK_NOPE
                kn = kv[:, h * QK_NOPE:(h + 1) * QK_NOPE]
                r = lax.rsqrt((jnp.sum(kn * kn, axis=-1, keepdims=True) + state["ss_r"]) * (1.0 / QK_HEAD) + EPS)
                k_ref[0, h, :, :LANES] = (kn * r * knw_ref[:, :LANES]).astype(BF16)
                k_ref[0, h, :, LANES:] = (state["kr_rot"] * r).astype(BF16)
                v_ref[0, h] = kv[:, v_off + h * V_HEAD:v_off + (h + 1) * V_HEAD].astype(BF16)
            return run

        steps = [kv_prep] + [k_head(h) for h in range(N_HEADS)]
        if kv_only:
            return steps

        def q_prep():
            cq = _rms(col("cq"), qanw_ref[...]).astype(BF16)
            state["qraw"] = jnp.dot(cq, wuq_ref[...], preferred_element_type=F32)

        def q_head(h):
            def run():
                qraw = state["qraw"]
                q0 = qraw[:, h * HEAD_SLOT:h * HEAD_SLOT + LANES]
                q1 = qraw[:, h * HEAD_SLOT + LANES:(h + 1) * HEAD_SLOT]
                ss = jnp.sum(q0 * q0, axis=-1, keepdims=True) + jnp.sum(q1 * q1, axis=-1, keepdims=True)
                r = lax.rsqrt(ss * (1.0 / QK_HEAD) + EPS) * Q_SCALE
                q_ref[0, h, :, :LANES] = (q0 * r * qnw_ref[:, :LANES]).astype(BF16)
                q_ref[0, h, :, LANES:] = rope(q1 * r * qnw_ref[:, LANES:]).astype(BF16)
            return run

        def conv_gates():
            gb_ref[0] = col("gb").astype(BF16)
            g_ref[0] = (col("gc") * col("u")).astype(BF16)

        def dft_group(g):
            def run():
                f = p_ref[:, pcols["f"][0] + g * LANES:pcols["f"][0] + (g + 1) * LANES].astype(BF16)
                res = jnp.dot(f, wdft_ref[...], preferred_element_type=F32)
                fab_ref[0, 0, :, g * LANES:(g + 1) * LANES] = res[:, :LANES].astype(BF16)
                fab_ref[0, 1, :, g * LANES:(g + 1) * LANES] = res[:, LANES:].astype(BF16)
            return run

        return (steps + [q_prep] + [q_head(h) for h in range(N_HEADS)] + [conv_gates] +
                [dft_group(g) for g in range(FOURIER_GROUPS)])

    _two_stage(t, n_total, project, finish, (p0_ref, p1_ref))


def _interleave(*streams):
    total = max(len(st) for st in streams)
    done = [0] * len(streams)
    for step in range(1, total + 1):
        for k, st in enumerate(streams):
            upto = -(-step * len(st) // total)
            while done[k] < upto:
                st[done[k]]()
                done[k] += 1


def _two_stage(t, n_total, first, second, bufs):
    steady = (t >= 1) & (t < n_total)
    for parity in (0, 1):
        mine = (t % 2) == parity
        new_buf, old_buf = bufs[parity], bufs[1 - parity]

        @pl.when(steady & mine)
        def _():
            _interleave(first(new_buf), second(old_buf))

        @pl.when((t == 0) & mine)
        def _():
            _interleave(first(new_buf))

        @pl.when((t == n_total) & mine)
        def _():
            _interleave(second(old_buf))


def _inproj(x, mods, lw, rope_c, rope_s, wdft, *, kv_only):
    nb, s, d = x.shape
    tm = min(s, 256)
    n_tiles = s // tm
    n_total = nb * n_tiles
    cols = lw["cols"]
    if kv_only:
        w_ckv = cols["ckv"][1] - cols["ckv"][0]
        pcols = {"ckv": (0, w_ckv), "kr": (w_ckv, w_ckv + LANES)}
    else:
        pcols = cols
    p_width = max(hi for _, hi in pcols.values())
    fw = wdft.shape[0] * FOURIER_GROUPS
    cw = cols["gb"][1] - cols["gb"][0]
    kernel = functools.partial(_inproj_kernel, cols=cols, pcols=pcols, kv_only=kv_only, n_total=n_total)
    cur = lambda t: jnp.minimum(t, n_total - 1)
    old = lambda t: jnp.maximum(t - 1, 0)
    head_spec = lambda w: pl.BlockSpec((1, N_HEADS, tm, w), lambda t: (old(t) // n_tiles, 0, old(t) % n_tiles, 0))
    tok_spec = lambda w: pl.BlockSpec((1, tm, w), lambda t: (old(t) // n_tiles, old(t) % n_tiles, 0))
    out_shape = [jax.ShapeDtypeStruct((nb, N_HEADS, s, HEAD_SLOT), BF16),
                 jax.ShapeDtypeStruct((nb, N_HEADS, s, V_HEAD), BF16)]
    out_specs = [head_spec(HEAD_SLOT), head_spec(V_HEAD)]
    if not kv_only:
        out_shape = ([jax.ShapeDtypeStruct((nb, N_HEADS, s, HEAD_SLOT), BF16)] + out_shape +
                     [jax.ShapeDtypeStruct((nb, s, cw), BF16), jax.ShapeDtypeStruct((nb, s, cw), BF16),
                      jax.ShapeDtypeStruct((nb, 2, s, fw), BF16)])
        out_specs = ([head_spec(HEAD_SLOT)] + out_specs +
                     [tok_spec(cw), tok_spec(cw),
                      pl.BlockSpec((1, 2, tm, fw), lambda t: (old(t) // n_tiles, 0, old(t) % n_tiles, 0))])
    weights = [lw["n1w"], lw["w_in"], lw["qanw"], lw["w_uq"], lw["kvanw"], lw["w_ukv"], lw["qnw"], lw["knw"]]
    in_specs = ([pl.BlockSpec((1, tm, d), lambda t: (cur(t) // n_tiles, cur(t) % n_tiles, 0)),
                 pl.BlockSpec((1, 6, d), lambda t: (cur(t) // n_tiles, 0, 0))] +
                [_const_spec(w.shape) for w in weights] +
                [pl.BlockSpec((tm, LANES), lambda t: (old(t) % n_tiles, 0)),
                 pl.BlockSpec((tm, LANES), lambda t: (old(t) % n_tiles, 0)),
                 _const_spec(wdft.shape)])
    return pl.pallas_call(
        kernel, grid=(n_total + 1,), in_specs=in_specs, out_specs=out_specs, out_shape=out_shape,
        scratch_shapes=[pltpu.VMEM((tm, p_width), F32), pltpu.VMEM((tm, p_width), F32)],
        compiler_params=_cparams(1), name="inproj_kv" if kv_only else "inproj",
    )(x, mods, lw["n1w"], lw["w_in"], lw["qanw"], lw["w_uq"], lw["kvanw"], lw["w_ukv"], lw["qnw"],
      lw["knw"], rope_c, rope_s, wdft)


_NT = (((1,), (1,)), ((), ()))


def _attn_kernel(q_ref, kc_ref, vc_ref, *rest, has_seq, sub):
    o_ref = rest[-1]
    key_refs = [kc_ref] + ([rest[0]] if has_seq else [])
    val_refs = [vc_ref] + ([rest[1]] if has_seq else [])

    def scores(r0):
        q = q_ref[0, 0, r0:r0 + sub, :]
        return [lax.dot_general(q, k[0, 0], _NT, preferred_element_type=F32) for k in key_refs]

    def finish(r0, s_parts):
        m = functools.reduce(jnp.maximum, [jnp.max(s, axis=-1, keepdims=True) for s in s_parts])
        p_parts = [jnp.exp2(s - m) for s in s_parts]
        l = sum(jnp.sum(p, axis=-1, keepdims=True) for p in p_parts)
        o = sum(jnp.dot(p.astype(BF16), v[0, 0], preferred_element_type=F32) for p, v in zip(p_parts, val_refs))
        o_ref[0, r0:r0 + sub, :] = (o / l).astype(BF16)

    starts = list(range(0, q_ref.shape[2], sub))
    pending = scores(starts[0])
    for nxt in starts[1:] + [None]:
        upcoming = scores(nxt) if nxt is not None else None
        finish(nxt - sub if nxt is not None else starts[-1], pending)
        pending = upcoming


def _attention(q, kc, vc, k=None, v=None):
    nb, nh, s, _ = q.shape
    n_ctx = kc.shape[2]
    has_seq = k is not None
    tq = min(s, 2048)
    kv_spec = lambda n, w: pl.BlockSpec((1, 1, n, w), lambda b, h, i: (b, h, 0, 0))
    in_specs = [pl.BlockSpec((1, 1, tq, HEAD_SLOT), lambda b, h, i: (b, h, i, 0)),
                kv_spec(n_ctx, HEAD_SLOT), kv_spec(n_ctx, V_HEAD)]
    args = [q, kc, vc]
    if has_seq:
        in_specs += [kv_spec(s, HEAD_SLOT), kv_spec(s, V_HEAD)]
        args += [k, v]
    return pl.pallas_call(
        functools.partial(_attn_kernel, has_seq=has_seq, sub=min(tq, 256)),
        grid=(nb, nh, s // tq), in_specs=in_specs,
        out_specs=pl.BlockSpec((1, tq, V_HEAD), lambda b, h, i: (b, i, h)),
        out_shape=jax.ShapeDtypeStruct((nb, s, nh * V_HEAD), BF16),
        compiler_params=_cparams(3), name="attention" if has_seq else "attention_ctx",
    )(*args)


def _seqdft_kernel(c_ref, s_ref, fab_ref, o_ref, *, scale):
    y = jnp.dot(c_ref[...], fab_ref[0, 0], preferred_element_type=F32)
    y = y + jnp.dot(s_ref[...], fab_ref[0, 1], preferred_element_type=F32)
    o_ref[0] = (y * scale).astype(BF16)


def _seqdft(dft_c, dft_s, fab, group_w):
    nb, _, s, fw = fab.shape
    tr = min(s, 1024)
    scale = 1.0 / math.sqrt(s * group_w)
    return pl.pallas_call(
        functools.partial(_seqdft_kernel, scale=scale),
        grid=(s // tr, nb),
        in_specs=[pl.BlockSpec((tr, s), lambda i, b: (i, 0)),
                  pl.BlockSpec((tr, s), lambda i, b: (i, 0)),
                  pl.BlockSpec((1, 2, s, fw), lambda i, b: (b, 0, 0, 0))],
        out_specs=pl.BlockSpec((1, tr, fw), lambda i, b: (b, i, 0)),
        out_shape=jax.ShapeDtypeStruct((nb, s, fw), BF16),
        compiler_params=_cparams(2), name="seqdft",
    )(dft_c, dft_s, fab)


HALO = 16
OUT_BLOCK = 512


def _mix_kernel(attn_ref, gb_ref, g_ref, gprev_ref, gnext_ref, fy_ref, x_ref, mod_ref, convw_ref,
                wout_ref, n2w_ref, wrcat_ref, wrhi_ref, br_ref, cntin_ref,
                x1_ref, h2_ref, route_ref, routet_ref, cnt_ref, carry_ref, hs0_ref, hs1_ref,
                *, tm, n_tiles, n_total, attn_w, conv_w):
    t = pl.program_id(0)
    i = t % n_tiles

    @pl.when(t == 0)
    def _():
        carry_ref[...] = cntin_ref[...]

    def mix_tile(hs_ref):
        state = {}

        def conv():
            g = g_ref[0].astype(F32)
            row = lax.broadcasted_iota(jnp.int32, g.shape, 0)
            prev_row = jnp.where(i > 0, gprev_ref[0, HALO - 1:HALO, :].astype(F32), 0.0)
            next_row = jnp.where(i < n_tiles - 1, gnext_ref[0, 0:1, :].astype(F32), 0.0)
            g_dn = jnp.where(row == 0, prev_row, pltpu.roll(g, 1, axis=0))
            g_up = jnp.where(row == tm - 1, next_row, pltpu.roll(g, tm - 1, axis=0))
            state["conv"] = (gb_ref[0].astype(F32) * (
                g_dn * convw_ref[0:1, :] + g * convw_ref[1:2, :] + g_up * convw_ref[2:3, :])).astype(BF16)

        def out_block(lo):
            def run():
                cs = slice(lo, lo + OUT_BLOCK)
                mix = jnp.dot(attn_ref[0], wout_ref[:attn_w, cs], preferred_element_type=F32)
                mix = mix + jnp.dot(state["conv"], wout_ref[attn_w:attn_w + conv_w, cs], preferred_element_type=F32)
                mix = mix + jnp.dot(fy_ref[0], wout_ref[attn_w + conv_w:, cs], preferred_element_type=F32)
                x1_ref[0, :, cs] = x_ref[0, :, cs] + mod_ref[0, 2:3, cs] * mix
            return run

        def norm2():
            h2 = _rms(x1_ref[0], n2w_ref[...]) * (1.0 + mod_ref[0, 4:5, :]) + mod_ref[0, 3:4, :]
            _pack_rows(h2_ref, h2)
            hs_ref[...] = h2

        return [conv] + [out_block(lo) for lo in range(0, x_ref.shape[2], OUT_BLOCK)] + [norm2]

    def route_tile(hs_ref):
        state = {}

        def logits():
            h2 = hs_ref[...]
            hi = h2.astype(BF16)
            lo = (h2 - hi.astype(F32)).astype(BF16)
            a = jnp.dot(hi, wrcat_ref[...], preferred_element_type=F32)
            state["logits"] = (a[:, :LANES] + a[:, LANES:] +
                               jnp.dot(lo, wrhi_ref[...], preferred_element_type=F32) + br_ref[...])

        def top():
            logits = state["logits"]
            lane = lax.broadcasted_iota(jnp.int32, logits.shape, 1)
            lane_f = lane.astype(F32)
            no_lane = float(LANES)
            is_group = lane < N_GROUPS
            gl = jnp.where(is_group, logits, NEG_BIG)
            gmax = jnp.max(gl, axis=-1, keepdims=True)
            gidx = jnp.min(jnp.where(gl == gmax, lane_f, no_lane), axis=-1, keepdims=True)
            g_p = 1.0 / jnp.sum(jnp.where(is_group, jnp.exp(gl - gmax), 0.0), axis=-1, keepdims=True)
            lane_group = ((lane - N_GROUPS) >> 3).astype(F32)
            in_group = (lane >= N_GROUPS) & (lane < N_GROUPS + N_EXPERTS) & (lane_group == gidx)
            el = jnp.where(in_group, logits, NEG_BIG)
            m1 = jnp.max(el, axis=-1, keepdims=True)
            i1 = jnp.min(jnp.where(el == m1, lane_f, no_lane), axis=-1, keepdims=True)
            el2 = jnp.where(lane_f == i1, NEG_BIG, el)
            m2 = jnp.max(el2, axis=-1, keepdims=True)
            i2 = jnp.min(jnp.where(el2 == m2, lane_f, no_lane), axis=-1, keepdims=True)
            e2 = jnp.exp(m2 - m1)
            state.update(lane=lane, lane_f=lane_f, i1=i1, i2=i2,
                         w1=g_p / (1.0 + e2), w2=g_p * e2 / (1.0 + e2))

        def ranks():
            lane, lane_f, i1, i2 = state["lane"], state["lane_f"], state["i1"], state["i2"]
            hit1, hit2 = lane_f == i1, lane_f == i2
            onehot = jnp.where(hit1 | hit2, 1.0, 0.0)
            ti = lax.broadcasted_iota(jnp.int32, (tm, tm), 0)
            tj = lax.broadcasted_iota(jnp.int32, (tm, tm), 1)
            before = jnp.where(tj < ti, 1.0, 0.0).astype(BF16)
            seen = jnp.dot(before, onehot.astype(BF16), preferred_element_type=F32) + carry_ref[...]
            r1 = jnp.sum(jnp.where(hit1, seen, 0.0), axis=-1, keepdims=True)
            r2 = jnp.sum(jnp.where(hit2, seen, 0.0), axis=-1, keepdims=True)
            carry_ref[...] = carry_ref[...] + jnp.sum(onehot, axis=0, keepdims=True)
            cnt_ref[...] = carry_ref[...]

            route = jnp.where(lane == 0, i1 - N_GROUPS, 0.0)
            route = jnp.where(lane == 1, i2 - N_GROUPS, route)
            route = jnp.where(lane == 2, r1, route)
            route = jnp.where(lane == 3, r2, route)
            route = jnp.where(lane == 4, state["w1"], route)
            route = jnp.where(lane == 5, state["w2"], route)
            route_ref[...] = route
            routet_ref[...] = route.T[:ROUTE_ROWS, :]

        return [logits, top, ranks]

    _two_stage(t, n_total, mix_tile, route_tile, (hs0_ref, hs1_ref))


def _mix(attn, gb, g, fy, x, mods, lw, cnt_in):
    nb, s, d = x.shape
    tm = min(s, 256)
    n_tiles = s // tm
    n_total = nb * n_tiles
    attn_w, conv_w = attn.shape[-1], gb.shape[-1]
    cur = lambda t: jnp.minimum(t, n_total - 1)
    old = lambda t: jnp.maximum(t - 1, 0)
    tok = lambda w: pl.BlockSpec((1, tm, w), lambda t: (cur(t) // n_tiles, cur(t) % n_tiles, 0))
    hb = tm // HALO
    in_specs = [tok(attn_w), tok(conv_w), tok(conv_w),
                pl.BlockSpec((1, HALO, conv_w),
                             lambda t: (cur(t) // n_tiles, jnp.maximum((cur(t) % n_tiles) * hb - 1, 0), 0)),
                pl.BlockSpec((1, HALO, conv_w),
                             lambda t: (cur(t) // n_tiles, jnp.minimum((cur(t) % n_tiles + 1) * hb, s // HALO - 1), 0)),
                tok(fy.shape[-1]), tok(d),
                pl.BlockSpec((1, 6, d), lambda t: (cur(t) // n_tiles, 0, 0))]
    weights = [lw["conv_w"], lw["w_out"], lw["n2w"], lw["wr_cat"], lw["wr_hi"], lw["b_r"], cnt_in]
    in_specs += [_const_spec(w.shape) for w in weights]
    n_tok = nb * s
    out_shape = [jax.ShapeDtypeStruct((nb, s, d), F32),
                 jax.ShapeDtypeStruct((n_tok * WORD_ROWS, LANES), U32),
                 jax.ShapeDtypeStruct((n_tok, LANES), F32),
                 jax.ShapeDtypeStruct((ROUTE_ROWS, n_tok), F32),
                 jax.ShapeDtypeStruct((1, LANES), F32)]
    out_specs = [tok(d),
                 pl.BlockSpec((tm * WORD_ROWS, LANES), lambda t: (cur(t), 0)),
                 pl.BlockSpec((tm, LANES), lambda t: (old(t), 0)),
                 pl.BlockSpec((ROUTE_ROWS, tm), lambda t: (0, old(t))),
                 pl.BlockSpec((1, LANES), lambda t: (0, 0))]
    return pl.pallas_call(
        functools.partial(_mix_kernel, tm=tm, n_tiles=n_tiles, n_total=n_total, attn_w=attn_w, conv_w=conv_w),
        grid=(n_total + 1,), in_specs=in_specs, out_specs=out_specs, out_shape=out_shape,
        scratch_shapes=[pltpu.VMEM((1, LANES), F32), pltpu.VMEM((tm, d), F32), pltpu.VMEM((tm, d), F32)],
        compiler_params=_cparams(1), name="mix_router",
    )(attn, gb, g, g, g, fy, x, mods, *weights)


def _row(ref, idx):
    return ref.at[pl.ds(pl.multiple_of(idx * WORD_ROWS, WORD_ROWS), WORD_ROWS), :]


DMA_UNROLL = 8
ZERO_ROWS = 128


def _row_dma_loop(n_rows, copies, method, alternate=False):
    def body(jo, carry):
        n = 0
        for u in range(DMA_UNROLL):
            for c in copies(jo * DMA_UNROLL + u):
                if method == "start":
                    c.start(priority=n % 2 if alternate else 0)
                else:
                    c.wait()
                n += 1
        return carry
    lax.fori_loop(0, n_rows // DMA_UNROLL, body, 0)


def _tile_slots(slots_kt, tm):
    n_tok = slots_kt.shape[1]
    return slots_kt.reshape(TOP_K, n_tok // tm, tm).transpose(1, 0, 2).reshape(n_tok // tm, 1, TOP_K * tm)


def _dispatch_kernel(lo_ref, n_ref, slot_ref, *refs, tm, n_main_tiles, has_ctx):
    if has_ctx:
        h2_ref, h2c_ref, xs_ref, zero_ref, sem, zsem = refs
    else:
        h2_ref, xs_ref, zero_ref, sem, zsem = refs
    i = pl.program_id(0)

    @pl.when(i == 0)
    def _():
        zero_ref[...] = jnp.zeros(zero_ref.shape, U32)

        def zero_copy(first, rows):
            return pltpu.make_async_copy(
                zero_ref.at[pl.ds(0, rows * WORD_ROWS), :],
                xs_ref.at[pl.ds(pl.multiple_of(first * WORD_ROWS, WORD_ROWS), rows * WORD_ROWS), :], zsem)

        def fill(e, carry):
            lo, n = lo_ref[e], n_ref[e]
            n_full = n // ZERO_ROWS

            def run(method):
                def full(c, cc):
                    getattr(zero_copy(lo + c * ZERO_ROWS, ZERO_ROWS), method)()
                    return cc
                lax.fori_loop(0, n_full, full, 0)
                first = lo + n_full * ZERO_ROWS
                p = ZERO_ROWS // 2
                while p >= 1:
                    @pl.when((n & p) != 0)
                    def _(first=first, p=p):
                        getattr(zero_copy(first, p), method)()
                    first = first + (n & p)
                    p //= 2
            run("start")
            run("wait")
            return carry

        lax.fori_loop(0, lo_ref.shape[0], fill, 0)

    def scatter(src_ref):
        copies = lambda j: [
            pltpu.make_async_copy(_row(src_ref, j), _row(xs_ref, slot_ref[0, 0, k * tm + j]), sem)
            for k in range(TOP_K)]
        _row_dma_loop(tm, copies, "start", alternate=True)
        _row_dma_loop(tm, copies, "wait")

    if has_ctx:
        @pl.when(i < n_main_tiles)
        def _():
            scatter(h2_ref)

        @pl.when(i >= n_main_tiles)
        def _():
            scatter(h2c_ref)
    else:
        scatter(h2_ref)


def _dispatch(slots_kt, fill_lo, fill_n, h2, h2c, n_slots):
    tm = MOE_BLOCK
    n_main_tiles = h2.shape[0] // (tm * WORD_ROWS)
    has_ctx = h2c is not None
    slots3 = _tile_slots(slots_kt, tm)
    n_tiles = slots3.shape[0]
    blk = (tm * WORD_ROWS, LANES)
    in_specs = [pl.BlockSpec((1, 1, tm * TOP_K), lambda i, lo, n: (i, 0, 0), memory_space=pltpu.SMEM),
                pl.BlockSpec(blk, lambda i, lo, n: (jnp.minimum(i, n_main_tiles - 1), 0))]
    args = [slots3, h2]
    if has_ctx:
        in_specs.append(pl.BlockSpec(blk, lambda i, lo, n: (jnp.maximum(i - n_main_tiles, 0), 0)))
        args.append(h2c)
    grid_spec = pltpu.PrefetchScalarGridSpec(
        num_scalar_prefetch=2, grid=(n_tiles,), in_specs=in_specs,
        out_specs=pl.BlockSpec(memory_space=pl.ANY),
        scratch_shapes=[pltpu.VMEM((ZERO_ROWS * WORD_ROWS, LANES), U32),
                        pltpu.SemaphoreType.DMA(()), pltpu.SemaphoreType.DMA(())])
    return pl.pallas_call(
        functools.partial(_dispatch_kernel, tm=tm, n_main_tiles=n_main_tiles, has_ctx=has_ctx),
        grid_spec=grid_spec,
        out_shape=jax.ShapeDtypeStruct((n_slots * WORD_ROWS, LANES), U32),
        compiler_params=_cparams(1), name="dispatch",
    )(fill_lo, fill_n, *args)


def _expert_kernel(bexp_ref, nused_ref, nxt_ref, par_ref, xs_ref, wg_hbm, wu_hbm, wd_hbm, y_ref,
                   xb0_ref, xb1_ref, yb0_ref, yb1_ref, wgu_ref, wdb_ref, wgf_ref, wuf_ref, wdf_ref, wsem,
                   *, layer, d_expert, n_blocks):
    s = pl.program_id(0)
    n_used = nused_ref[0]
    half = xb0_ref.shape[1] // 2

    def unpack(xb_ref):
        def rows(r):
            def run():
                lo, hi = _unpack_pair(xs_ref[pl.ds(r, MOE_BLOCK, stride=WORD_ROWS), :])
                xb_ref[:, r * LANES:(r + 1) * LANES] = lo.astype(BF16)
                xb_ref[:, half + r * LANES:half + (r + 1) * LANES] = hi.astype(BF16)
            return run
        return [rows(r) for r in range(WORD_ROWS)]

    def mlp(xb_ref, yb_ref):
        state = {}

        def gate_up():
            gu = jnp.dot(xb_ref[...], wgu_ref[...], preferred_element_type=F32)
            gate, up = gu[:, :d_expert], gu[:, d_expert:]
            state["hid"] = (gate * jax.nn.sigmoid(gate) * up).astype(BF16)

        def down(lo):
            def run():
                yb_ref[:, lo:lo + half] = jnp.dot(state["hid"], wdb_ref[:, lo:lo + half], preferred_element_type=F32)
            return run

        return [gate_up, down(0), down(half)]

    def store(yb_ref):
        def rows(r):
            def run():
                y_ref[pl.ds(r, MOE_BLOCK, stride=WORD_ROWS), :] = _pack_pair(
                    yb_ref[:, r * LANES:(r + 1) * LANES], yb_ref[:, half + r * LANES:half + (r + 1) * LANES])
            return run
        return [rows(r) for r in range(WORD_ROWS)]

    def weight_copies(e, buf):
        return [pltpu.make_async_copy(src.at[layer, e], dst.at[buf], wsem.at[buf])
                for src, dst in ((wg_hbm, wgf_ref), (wu_hbm, wuf_ref), (wd_hbm, wdf_ref))]

    run_unpack = s < n_used
    run_mlp = (s >= 1) & (s <= n_used)
    run_store = (s >= 2) & (s <= n_used + 1)
    steady = (s >= 2) & (s < n_used)

    mlp_block = jnp.clip(s - 1, 0, n_blocks - 1)
    new_expert = (s == 1) | (bexp_ref[mlp_block] != bexp_ref[jnp.maximum(mlp_block - 1, 0)])

    @pl.when(s == 0)
    def _():
        for c in weight_copies(bexp_ref[0], 0):
            c.start()

    @pl.when(run_mlp & new_expert)
    def _():
        e = bexp_ref[mlp_block]
        buf = par_ref[e]
        for c in weight_copies(e, buf):
            c.wait()
        nxt = nxt_ref[e]

        @pl.when(nxt < N_EXPERTS)
        def _():
            for c in weight_copies(nxt, 1 - buf):
                c.start()

        wgu_ref[:, :d_expert] = wgf_ref[buf].astype(BF16)
        wgu_ref[:, d_expert:] = wuf_ref[buf].astype(BF16)
        wdb_ref[...] = wdf_ref[buf].astype(BF16)

    for parity, (xb_new, xb_cur, yb_cur, yb_old) in enumerate(
            [(xb0_ref, xb1_ref, yb1_ref, yb0_ref), (xb1_ref, xb0_ref, yb0_ref, yb1_ref)]):
        mine = (s % 2) == parity

        @pl.when(steady & mine)
        def _():
            _interleave(mlp(xb_cur, yb_cur), unpack(xb_new), store(yb_old))

        @pl.when(jnp.logical_not(steady) & mine)
        def _():
            @pl.when(run_unpack)
            def _():
                _interleave(unpack(xb_new))

            @pl.when(run_mlp)
            def _():
                _interleave(mlp(xb_cur, yb_cur))

            @pl.when(run_store)
            def _():
                _interleave(store(yb_old))

    @pl.when(s >= n_used + 2)
    def _():
        y_ref[...] = jnp.zeros(y_ref.shape, U32)


def _experts(layer, block_exp, n_used, next_exp, run_par, xs, w_gate, w_up, w_down):
    _, _, d, d_expert = w_gate.shape
    n_blocks = block_exp.shape[0]
    blk = (MOE_BLOCK * WORD_ROWS, LANES)
    hbm = pl.BlockSpec(memory_space=pl.ANY)
    grid_spec = pltpu.PrefetchScalarGridSpec(
        num_scalar_prefetch=4, grid=(n_blocks + 2,),
        in_specs=[pl.BlockSpec(blk, lambda s, be, nu, nx, pr: (jnp.minimum(s, nu[0] - 1), 0)), hbm, hbm, hbm],
        out_specs=pl.BlockSpec(blk, lambda s, be, nu, nx, pr: (jnp.maximum(s - 2, 0), 0)),
        scratch_shapes=[pltpu.VMEM((MOE_BLOCK, d), BF16), pltpu.VMEM((MOE_BLOCK, d), BF16),
                        pltpu.VMEM((MOE_BLOCK, d), F32), pltpu.VMEM((MOE_BLOCK, d), F32),
                        pltpu.VMEM((d, 2 * d_expert), BF16),
                        pltpu.VMEM((d_expert, d), BF16),
                        pltpu.VMEM((2, d, d_expert), F32), pltpu.VMEM((2, d, d_expert), F32),
                        pltpu.VMEM((2, d_expert, d), F32),
                        pltpu.SemaphoreType.DMA((2,))])
    return pl.pallas_call(
        functools.partial(_expert_kernel, layer=layer, d_expert=d_expert, n_blocks=n_blocks),
        grid_spec=grid_spec,
        out_shape=jax.ShapeDtypeStruct(xs.shape, U32),
        compiler_params=_cparams(1), name="experts",
    )(block_exp, n_used, next_exp, run_par, xs, w_gate, w_up, w_down)


def _combine_kernel(slot_ref, slotn_ref, x1_ref, mod_ref, route_ref, y_ref, o_ref, ybuf_ref, sem, *, tm, n_total):
    t = pl.program_id(0)
    half_rows = TOP_K * tm
    cur = t % 2

    def gathers(slots, buf):
        return lambda j: [
            pltpu.make_async_copy(_row(y_ref, slots[0, 0, k * tm + j]), _row(ybuf_ref, buf * half_rows + k * tm + j),
                                  sem.at[buf])
            for k in range(TOP_K)]

    @pl.when(t == 0)
    def _():
        _row_dma_loop(tm, gathers(slot_ref, 0), "start")

    @pl.when(t + 1 < n_total)
    def _():
        _row_dma_loop(tm, gathers(slotn_ref, 1 - cur), "start")

    _row_dma_loop(tm, gathers(slot_ref, cur), "wait")

    w1, w2 = route_ref[:, 4:5], route_ref[:, 5:6]
    base = cur * half_rows * WORD_ROWS
    half = x1_ref.shape[2] // 2
    for r in range(WORD_ROWS):
        a_lo, a_hi = _unpack_pair(ybuf_ref[pl.ds(base + r, tm, stride=WORD_ROWS), :])
        b_lo, b_hi = _unpack_pair(ybuf_ref[pl.ds(base + tm * WORD_ROWS + r, tm, stride=WORD_ROWS), :])
        for off, ya, yb in ((0, a_lo, b_lo), (half, a_hi, b_hi)):
            lanes = slice(off + r * LANES, off + (r + 1) * LANES)
            o_ref[0, :, lanes] = x1_ref[0, :, lanes] + mod_ref[0, 5:6, lanes] * (ya * w1 + yb * w2)


def _combine(slots_kt, x1, mods, route, y):
    nb, s, d = x1.shape
    tm = min(s, MOE_BLOCK)
    n_tiles = s // tm
    n_total = nb * n_tiles
    slots3 = _tile_slots(slots_kt, tm)
    slot_spec = lambda nxt: pl.BlockSpec((1, 1, tm * TOP_K), lambda t: (jnp.minimum(t + nxt, n_total - 1), 0, 0),
                                         memory_space=pltpu.SMEM)
    return pl.pallas_call(
        functools.partial(_combine_kernel, tm=tm, n_total=n_total),
        grid=(n_total,),
        in_specs=[slot_spec(0), slot_spec(1),
                  pl.BlockSpec((1, tm, d), lambda t: (t // n_tiles, t % n_tiles, 0)),
                  pl.BlockSpec((1, 6, d), lambda t: (t // n_tiles, 0, 0)),
                  pl.BlockSpec((tm, LANES), lambda t: (t, 0)),
                  pl.BlockSpec(memory_space=pl.ANY)],
        out_specs=pl.BlockSpec((1, tm, d), lambda t: (t // n_tiles, t % n_tiles, 0)),
        out_shape=jax.ShapeDtypeStruct((nb, s, d), F32),
        scratch_shapes=[pltpu.VMEM((2 * TOP_K * tm * WORD_ROWS, LANES), U32), pltpu.SemaphoreType.DMA((2,))],
        compiler_params=_cparams(1), name="combine",
    )(slots3, slots3, x1, mods, route, y)


def _layer_weights(l, d, norm1_w, norm2_w, w_in, q_a_norm_w, w_uq, kv_a_norm_w, w_ukv, q_norm_w, k_norm_w,
                   conv_w, w_out, w_rg, b_rg, w_re, b_re):
    q_lora, kv_lora = q_a_norm_w.shape[1], kv_a_norm_w.shape[1]
    conv_cols = conv_w.shape[2]
    in_cols = w_in.shape[2]
    four_cols = in_cols - (q_lora + kv_lora + QK_ROPE + 3 * conv_cols)
    o_ckv = q_lora
    o_kr = o_ckv + kv_lora
    o_gb = o_kr + QK_ROPE
    o_gc, o_u, o_f = o_gb + conv_cols, o_gb + 2 * conv_cols, o_gb + 3 * conv_cols
    wi = w_in[l]
    w_in_p = jnp.concatenate([wi[:, :o_kr], wi[:, o_gb:], wi[:, o_kr:o_gb],
                              jnp.zeros((d, LANES - QK_ROPE), F32)], axis=1).astype(BF16)
    names, widths = ["cq", "ckv", "gb", "gc", "u", "f", "kr"], [q_lora, kv_lora, conv_cols, conv_cols, conv_cols, four_cols, LANES]
    cols, o = {}, 0
    for n, w in zip(names, widths):
        cols[n] = (o, o + w)
        o += w
    pad_head = lambda w: jnp.pad(w, [(0, 0)] * (w.ndim - 1) + [(0, HEAD_SLOT - QK_HEAD)])
    w_uq_p = pad_head(w_uq[l].reshape(q_lora, N_HEADS, QK_HEAD)).reshape(q_lora, N_HEADS * HEAD_SLOT).astype(BF16)
    w_ukv_p = (w_ukv[l].reshape(kv_lora, N_HEADS, 2, QK_NOPE).transpose(0, 2, 1, 3)
               .reshape(kv_lora, 2 * N_HEADS * QK_NOPE).astype(BF16))
    w_r = jnp.concatenate([w_rg[l], w_re[l], jnp.zeros((d, LANES - N_GROUPS - N_EXPERTS), F32)], axis=1)
    wr_hi = w_r.astype(BF16)
    wr_lo = (w_r - wr_hi.astype(F32)).astype(BF16)
    b_r = jnp.concatenate([b_rg[l], b_re[l], jnp.zeros((LANES - N_GROUPS - N_EXPERTS,), F32)]).reshape(1, LANES)
    return dict(cols=cols, n1w=norm1_w[l].reshape(1, d), n2w=norm2_w[l].reshape(1, d), w_in=w_in_p,
                qanw=q_a_norm_w[l].reshape(1, q_lora), w_uq=w_uq_p, kvanw=kv_a_norm_w[l].reshape(1, kv_lora),
                w_ukv=w_ukv_p, qnw=pad_head(q_norm_w[l]).reshape(1, HEAD_SLOT),
                knw=pad_head(k_norm_w[l]).reshape(1, HEAD_SLOT), conv_w=conv_w[l], w_out=w_out[l].astype(BF16),
                wr_cat=jnp.concatenate([wr_hi, wr_lo], axis=1), wr_hi=wr_hi, b_r=b_r)


def _rope_tables(n_tok):
    freqs = QK_ROPE // 4
    pos = jnp.arange(n_tok)
    row = (pos // GRID_W).astype(F32)
    colp = (pos % GRID_W).astype(F32)
    inv = ROPE_THETA ** (-jnp.arange(freqs, dtype=F32) / freqs)
    ar, ac = row[:, None] * inv, colp[:, None] * inv
    zeros = jnp.zeros((n_tok, LANES - QK_ROPE), F32)
    rope_c = jnp.concatenate([jnp.cos(ar), jnp.cos(ar), jnp.cos(ac), jnp.cos(ac), zeros], axis=1)
    rope_s = jnp.concatenate([-jnp.sin(ar), jnp.sin(ar), -jnp.sin(ac), jnp.sin(ac), zeros], axis=1)
    return rope_c, rope_s


def _identity_rope(n_tok):
    ones = jnp.concatenate([jnp.ones((n_tok, QK_ROPE), F32), jnp.zeros((n_tok, LANES - QK_ROPE), F32)], axis=1)
    return ones, jnp.zeros((n_tok, LANES), F32)


def _dft_tables(n):
    idx = jnp.arange(n, dtype=jnp.int32)
    ang = ((idx[:, None] * idx[None, :]) % n).astype(F32) * (2.0 * math.pi / n)
    return jnp.cos(ang), jnp.sin(ang)


def _slot_plan(route_t, counts):
    n_tok = route_t.shape[1]
    expert = route_t[0:TOP_K].astype(jnp.int32)
    rank = route_t[TOP_K:2 * TOP_K].astype(jnp.int32)
    padded = (counts + MOE_BLOCK - 1) // MOE_BLOCK * MOE_BLOCK
    pad_ends = jnp.cumsum(padded)
    pad_starts = pad_ends - padded
    base = functools.reduce(lambda acc, e: jnp.where(expert == e, pad_starts[e], acc), range(N_EXPERTS),
                            jnp.zeros_like(expert))
    slots = base + rank
    n_blocks = -(-(n_tok * TOP_K) // MOE_BLOCK) + N_EXPERTS
    n_slots = n_blocks * MOE_BLOCK
    block_start = jnp.arange(n_blocks, dtype=jnp.int32) * MOE_BLOCK
    block_exp = jnp.sum((pad_ends[None, :] <= block_start[:, None]).astype(jnp.int32), axis=1)
    block_exp = jnp.minimum(block_exp, N_EXPERTS - 1)
    n_used = (pad_ends[-1] // MOE_BLOCK).astype(jnp.int32)
    last_exp = block_exp[jnp.maximum(n_used - 1, 0)]
    block_exp = jnp.where(jnp.arange(n_blocks) < n_used, block_exp, last_exp)
    fill_lo = jnp.concatenate([pad_starts + counts, pad_ends[-1:]]).astype(jnp.int32)
    fill_n = jnp.concatenate([padded - counts, n_slots - pad_ends[-1:]]).astype(jnp.int32)
    ids = jnp.arange(N_EXPERTS, dtype=jnp.int32)
    live_id = jnp.where(counts > 0, ids, N_EXPERTS)
    live_from = lax.cummin(live_id[::-1])[::-1]
    next_exp = jnp.concatenate([live_from[1:], jnp.full((1,), N_EXPERTS, jnp.int32)])
    run_par = (jnp.cumsum(counts > 0) - (counts > 0)).astype(jnp.int32) & 1
    return (slots.astype(jnp.int32), block_exp, n_used.reshape(1), next_exp.astype(jnp.int32), run_par,
            fill_lo, fill_n, n_slots)


def kernel(x, c, ctx, c_ctx, w_mod, b_mod, norm1_w, norm2_w, w_in, q_a_norm_w, w_uq, kv_a_norm_w, w_ukv, q_norm_w, k_norm_w, conv_w, w_out, w_router_group, b_router_group, w_router_expert, b_router_expert, w_gate, w_up, w_down):
    nb, s, d = x.shape
    n_ctx = ctx.shape[1]
    depth = w_mod.shape[0]
    group_w = conv_w.shape[2] // FOURIER_GROUPS

    rows = -(-(nb + 1) // 8) * 8
    c_all = jnp.concatenate([c, c_ctx[None, :], jnp.zeros((rows - nb - 1, d), F32)], axis=0)
    mod_all = _modulation(c_all, w_mod, b_mod)

    rope_c, rope_s = _rope_tables(s)
    id_c, id_s = _identity_rope(n_ctx)
    cc, sc = _dft_tables(group_w)
    wdft = jnp.concatenate([cc, -sc], axis=1).astype(BF16)
    seq_c, seq_s = (t.astype(BF16) for t in _dft_tables(s))
    ctx_c, ctx_s = (t.astype(BF16) for t in _dft_tables(n_ctx))

    xc = ctx
    for l in range(depth):
        last = l == depth - 1
        lw = _layer_weights(l, d, norm1_w, norm2_w, w_in, q_a_norm_w, w_uq, kv_a_norm_w, w_ukv, q_norm_w,
                            k_norm_w, conv_w, w_out, w_router_group, b_router_group, w_router_expert,
                            b_router_expert)
        mods = mod_all[l, :nb].reshape(nb, 6, d)
        mods_c = jnp.broadcast_to(mod_all[l, nb].reshape(1, 6, d), (nb, 6, d))

        q, k, v, gb, g, fab = _inproj(x, mods, lw, rope_c, rope_s, wdft, kv_only=False)
        if last:
            kc, vc = _inproj(xc, mods_c, lw, id_c, id_s, wdft, kv_only=True)
        else:
            qc, kc, vc, gbc, gc, fabc = _inproj(xc, mods_c, lw, id_c, id_s, wdft, kv_only=False)
        attn = _attention(q, kc, vc, k, v)
        fy = _seqdft(seq_c, seq_s, fab, group_w)
        zero_cnt = jnp.zeros((1, LANES), F32)
        x1, h2, route, route_t, cnt = _mix(attn, gb, g, fy, x, mods, lw, zero_cnt)
        if not last:
            attn_c = _attention(qc, kc, vc)
            fyc = _seqdft(ctx_c, ctx_s, fabc, group_w)
            x1c, h2c, route_c, route_tc, cnt = _mix(attn_c, gbc, gc, fyc, xc, mods_c, lw, cnt)
            route_t = jnp.concatenate([route_t, route_tc], axis=1)
        else:
            h2c = None

        counts = cnt[0, N_GROUPS:N_GROUPS + N_EXPERTS].astype(jnp.int32)
        slots, block_exp, n_used, next_exp, run_par, fill_lo, fill_n, n_slots = _slot_plan(route_t, counts)
        xs = _dispatch(slots, fill_lo, fill_n, h2, h2c, n_slots)
        y = _experts(l, block_exp, n_used, next_exp, run_par, xs, w_gate, w_up, w_down)
        n_main = nb * s
        x = _combine(slots[:, :n_main], x1, mods, route, y)
        if not last:
            xc = _combine(slots[:, n_main:], x1c, mods_c, route_c, y)
    return x
```

```python
import functools
import math

import jax
import jax.numpy as jnp
from jax import lax
from jax.experimental import pallas as pl
from jax.experimental.pallas import tpu as pltpu

F32 = jnp.float32
BF16 = jnp.bfloat16

EPS = 1e-6
N_HEADS = 8
QK_NOPE = 128
QK_ROPE = 64
QK_HEAD = QK_NOPE + QK_ROPE
V_HEAD = 128
GRID_W = 64
ROPE_THETA = 10000.0
ATTN_SCALE = QK_HEAD ** -0.5
Q_SCALE = ATTN_SCALE * math.log2(math.e)
CONV_K = 3
FOURIER_GROUPS = 4
N_GROUPS = 8
EXPERTS_PER_GROUP = 8
N_EXPERTS = N_GROUPS * EXPERTS_PER_GROUP
TOP_K = 2
MOE_BLOCK = 256

LANES = 128
HEAD_SLOT = 2 * LANES
WORD_ROWS = 8
ROUTE_ROWS = 8
U32 = jnp.uint32
PROJ_BLOCK = 512
VMEM_LIMIT = 56 * 1024 * 1024
NEG_BIG = -1e30


def _cparams(n_axes):
    return pltpu.CompilerParams(dimension_semantics=("arbitrary",) * n_axes,
                                vmem_limit_bytes=VMEM_LIMIT)


def _pack_pair(lo, hi):
    lo_bits = lax.bitcast_convert_type(lo.astype(BF16).astype(F32), U32) >> 16
    hi_bits = lax.bitcast_convert_type(hi.astype(BF16).astype(F32), U32) & U32(0xFFFF0000)
    return hi_bits | lo_bits


def _unpack_pair(words):
    lo = lax.bitcast_convert_type(words << 16, F32)
    hi = lax.bitcast_convert_type(words & U32(0xFFFF0000), F32)
    return lo, hi


def _pack_rows(dst_ref, val):
    rows, d = val.shape
    half = d // 2
    for r in range(WORD_ROWS):
        dst_ref[pl.ds(r, rows, stride=WORD_ROWS), :] = _pack_pair(
            val[:, r * LANES:(r + 1) * LANES], val[:, half + r * LANES:half + (r + 1) * LANES])


def _const_spec(shape):
    nd = len(shape)
    return pl.BlockSpec(shape, lambda *_: (0,) * nd, pipeline_mode=pl.Buffered(1))


def _mod_kernel(c_ref, w_ref, b_ref, o_ref):
    c = c_ref[...]
    a = (c * jax.nn.sigmoid(c)).astype(BF16)
    o_ref[0] = jnp.dot(a, w_ref[0].astype(BF16), preferred_element_type=F32) + b_ref[0]


def _modulation(c_all, w_mod, b_mod):
    n_layers, d, n_out = w_mod.shape
    rows = c_all.shape[0]
    tn = 1024
    return pl.pallas_call(
        _mod_kernel,
        grid=(n_layers, n_out // tn),
        in_specs=[pl.BlockSpec((rows, d), lambda l, j: (0, 0)),
                  pl.BlockSpec((1, d, tn), lambda l, j: (l, 0, j)),
                  pl.BlockSpec((1, 1, tn), lambda l, j: (l, 0, j))],
        out_specs=pl.BlockSpec((1, rows, tn), lambda l, j: (l, 0, j)),
        out_shape=jax.ShapeDtypeStruct((n_layers, rows, n_out), F32),
        compiler_params=_cparams(2),
        name="modulation",
    )(c_all, w_mod, b_mod.reshape(n_layers, 1, n_out))


def _rms(t, w):
    return t * lax.rsqrt(jnp.mean(t * t, axis=-1, keepdims=True) + EPS) * w


def _rope_partner(t):
    lane = lax.broadcasted_iota(jnp.int32, t.shape, 1)
    ahead = pltpu.roll(t, LANES - 16, axis=1)
    behind = pltpu.roll(t, 16, axis=1)
    return jnp.where((lane & 16) == 0, ahead, behind)


def _inproj_kernel(x_ref, mod_ref, n1w_ref, win_ref, qanw_ref, wuq_ref, kvanw_ref, wukv_ref,
                   qnw_ref, knw_ref, rc_ref, rs_ref, wdft_ref, *rest, cols, pcols, kv_only, n_total):
    if kv_only:
        k_ref, v_ref, p0_ref, p1_ref = rest
    else:
        q_ref, k_ref, v_ref, gb_ref, g_ref, fab_ref, p0_ref, p1_ref = rest
    t = pl.program_id(0)

    def project(p_ref):
        state = {}

        def norm():
            shift, scale = mod_ref[0, 0:1, :], mod_ref[0, 1:2, :]
            state["hb"] = (_rms(x_ref[0], n1w_ref[...]) * (1.0 + scale) + shift).astype(BF16)

        def block(src, dst):
            def run():
                p_ref[:, dst[0]:dst[1]] = jnp.dot(state["hb"], win_ref[:, src[0]:src[1]],
                                                   preferred_element_type=F32)
            return run

        if kv_only:
            return [norm] + [block(cols[n], pcols[n]) for n in pcols]
        width = win_ref.shape[1]
        return [norm] + [block((lo, min(lo + PROJ_BLOCK, width)), (lo, min(lo + PROJ_BLOCK, width)))
                         for lo in range(0, width, PROJ_BLOCK)]

    def finish(p_ref):
        col = lambda n: p_ref[:, pcols[n][0]:pcols[n][1]]
        state = {}

        def rope(v):
            return v * rc_ref[...] + _rope_partner(v) * rs_ref[...]

        def kv_prep():
            ckv = _rms(col("ckv"), kvanw_ref[...]).astype(BF16)
            state["kv"] = jnp.dot(ckv, wukv_ref[...], preferred_element_type=F32)
            k_r = col("kr")
            state["ss_r"] = jnp.sum(k_r * k_r, axis=-1, keepdims=True)
            state["kr_rot"] = rope(k_r * knw_ref[:, LANES:])

        def k_head(h):
            def run():
                kv, v_off = state["kv"], N_HEADS * QK_NOPE
                kn = kv[:, h * QK_NOPE:(h + 1) * QK_NOPE]
                r = lax.rsqrt((jnp.sum(kn * kn, axis=-1, keepdims=True) + state["ss_r"]) * (1.0 / QK_HEAD) + EPS)
                k_ref[0, h, :, :LANES] = (kn * r * knw_ref[:, :LANES]).astype(BF16)
                k_ref[0, h, :, LANES:] = (state["kr_rot"] * r).astype(BF16)
                v_ref[0, h] = kv[:, v_off + h * V_HEAD:v_off + (h + 1) * V_HEAD].astype(BF16)
            return run

        steps = [kv_prep] + [k_head(h) for h in range(N_HEADS)]
        if kv_only:
            return steps

        def q_prep():
            cq = _rms(col("cq"), qanw_ref[...]).astype(BF16)
            state["qraw"] = jnp.dot(cq, wuq_ref[...], preferred_element_type=F32)

        def q_head(h):
            def run():
                qraw = state["qraw"]
                q0 = qraw[:, h * HEAD_SLOT:h * HEAD_SLOT + LANES]
                q1 = qraw[:, h * HEAD_SLOT + LANES:(h + 1) * HEAD_SLOT]
                ss = jnp.sum(q0 * q0, axis=-1, keepdims=True) + jnp.sum(q1 * q1, axis=-1, keepdims=True)
                r = lax.rsqrt(ss * (1.0 / QK_HEAD) + EPS) * Q_SCALE
                q_ref[0, h, :, :LANES] = (q0 * r * qnw_ref[:, :LANES]).astype(BF16)
                q_ref[0, h, :, LANES:] = rope(q1 * r * qnw_ref[:, LANES:]).astype(BF16)
            return run

        def conv_gates():
            gb_ref[0] = col("gb").astype(BF16)
            g_ref[0] = (col("gc") * col("u")).astype(BF16)

        def dft_group(g):
            def run():
                f = p_ref[:, pcols["f"][0] + g * LANES:pcols["f"][0] + (g + 1) * LANES].astype(BF16)
                res = jnp.dot(f, wdft_ref[...], preferred_element_type=F32)
                fab_ref[0, 0, :, g * LANES:(g + 1) * LANES] = res[:, :LANES].astype(BF16)
                fab_ref[0, 1, :, g * LANES:(g + 1) * LANES] = res[:, LANES:].astype(BF16)
            return run

        return (steps + [q_prep] + [q_head(h) for h in range(N_HEADS)] + [conv_gates] +
                [dft_group(g) for g in range(FOURIER_GROUPS)])

    _two_stage(t, n_total, project, finish, (p0_ref, p1_ref))


def _interleave(*streams):
    total = max(len(st) for st in streams)
    done = [0] * len(streams)
    for step in range(1, total + 1):
        for k, st in enumerate(streams):
            upto = -(-step * len(st) // total)
            while done[k] < upto:
                st[done[k]]()
                done[k] += 1


def _two_stage(t, n_total, first, second, bufs):
    steady = (t >= 1) & (t < n_total)
    for parity in (0, 1):
        mine = (t % 2) == parity
        new_buf, old_buf = bufs[parity], bufs[1 - parity]

        @pl.when(steady & mine)
        def _():
            _interleave(first(new_buf), second(old_buf))

        @pl.when((t == 0) & mine)
        def _():
            _interleave(first(new_buf))

        @pl.when((t == n_total) & mine)
        def _():
            _interleave(second(old_buf))


def _inproj(x, mods, lw, rope_c, rope_s, wdft, *, kv_only):
    nb, s, d = x.shape
    tm = min(s, 256)
    n_tiles = s // tm
    n_total = nb * n_tiles
    cols = lw["cols"]
    if kv_only:
        w_ckv = cols["ckv"][1] - cols["ckv"][0]
        pcols = {"ckv": (0, w_ckv), "kr": (w_ckv, w_ckv + LANES)}
    else:
        pcols = cols
    p_width = max(hi for _, hi in pcols.values())
    fw = wdft.shape[0] * FOURIER_GROUPS
    cw = cols["gb"][1] - cols["gb"][0]
    kernel = functools.partial(_inproj_kernel, cols=cols, pcols=pcols, kv_only=kv_only, n_total=n_total)
    cur = lambda t: jnp.minimum(t, n_total - 1)
    old = lambda t: jnp.maximum(t - 1, 0)
    head_spec = lambda w: pl.BlockSpec((1, N_HEADS, tm, w), lambda t: (old(t) // n_tiles, 0, old(t) % n_tiles, 0))
    tok_spec = lambda w: pl.BlockSpec((1, tm, w), lambda t: (old(t) // n_tiles, old(t) % n_tiles, 0))
    out_shape = [jax.ShapeDtypeStruct((nb, N_HEADS, s, HEAD_SLOT), BF16),
                 jax.ShapeDtypeStruct((nb, N_HEADS, s, V_HEAD), BF16)]
    out_specs = [head_spec(HEAD_SLOT), head_spec(V_HEAD)]
    if not kv_only:
        out_shape = ([jax.ShapeDtypeStruct((nb, N_HEADS, s, HEAD_SLOT), BF16)] + out_shape +
                     [jax.ShapeDtypeStruct((nb, s, cw), BF16), jax.ShapeDtypeStruct((nb, s, cw), BF16),
                      jax.ShapeDtypeStruct((nb, 2, s, fw), BF16)])
        out_specs = ([head_spec(HEAD_SLOT)] + out_specs +
                     [tok_spec(cw), tok_spec(cw),
                      pl.BlockSpec((1, 2, tm, fw), lambda t: (old(t) // n_tiles, 0, old(t) % n_tiles, 0))])
    weights = [lw["n1w"], lw["w_in"], lw["qanw"], lw["w_uq"], lw["kvanw"], lw["w_ukv"], lw["qnw"], lw["knw"]]
    in_specs = ([pl.BlockSpec((1, tm, d), lambda t: (cur(t) // n_tiles, cur(t) % n_tiles, 0)),
                 pl.BlockSpec((1, 6, d), lambda t: (cur(t) // n_tiles, 0, 0))] +
                [_const_spec(w.shape) for w in weights] +
                [pl.BlockSpec((tm, LANES), lambda t: (old(t) % n_tiles, 0)),
                 pl.BlockSpec((tm, LANES), lambda t: (old(t) % n_tiles, 0)),
                 _const_spec(wdft.shape)])
    return pl.pallas_call(
        kernel, grid=(n_total + 1,), in_specs=in_specs, out_specs=out_specs, out_shape=out_shape,
        scratch_shapes=[pltpu.VMEM((tm, p_width), F32), pltpu.VMEM((tm, p_width), F32)],
        compiler_params=_cparams(1), name="inproj_kv" if kv_only else "inproj",
    )(x, mods, lw["n1w"], lw["w_in"], lw["qanw"], lw["w_uq"], lw["kvanw"], lw["w_ukv"], lw["qnw"],
      lw["knw"], rope_c, rope_s, wdft)


_NT = (((1,), (1,)), ((), ()))
ATTN_HEADS_PER_STEP = 2


def _attn_kernel(q_ref, kc_ref, vc_ref, *rest, has_seq, sub):
    o_ref = rest[-1]
    key_refs = [kc_ref] + ([rest[0]] if has_seq else [])
    val_refs = [vc_ref] + ([rest[1]] if has_seq else [])

    def scores(h, r0):
        q = q_ref[0, h, r0:r0 + sub, :]
        return [lax.dot_general(q, k[0, h], _NT, preferred_element_type=F32) for k in key_refs]

    def finish(h, r0, s_parts):
        m = functools.reduce(jnp.maximum, [jnp.max(s, axis=-1, keepdims=True) for s in s_parts])
        p_parts = [jnp.exp2(s - m) for s in s_parts]
        l = sum(jnp.sum(p, axis=-1, keepdims=True) for p in p_parts)
        o = sum(jnp.dot(p.astype(BF16), v[0, h], preferred_element_type=F32) for p, v in zip(p_parts, val_refs))
        o_ref[0, r0:r0 + sub, h * V_HEAD:(h + 1) * V_HEAD] = (o / l).astype(BF16)

    items = [(h, r0) for h in range(q_ref.shape[1]) for r0 in range(0, q_ref.shape[2], sub)]
    pending = scores(*items[0])
    for n, item in enumerate(items):
        upcoming = scores(*items[n + 1]) if n + 1 < len(items) else None
        finish(*item, pending)
        pending = upcoming


def _attention(q, kc, vc, k=None, v=None):
    nb, nh, s, _ = q.shape
    n_ctx = kc.shape[2]
    has_seq = k is not None
    tq = min(s, 2048)
    hp = ATTN_HEADS_PER_STEP if has_seq else nh
    kv_spec = lambda n, w: pl.BlockSpec((1, hp, n, w), lambda b, h, i: (b, h, 0, 0))
    in_specs = [pl.BlockSpec((1, hp, tq, HEAD_SLOT), lambda b, h, i: (b, h, i, 0)),
                kv_spec(n_ctx, HEAD_SLOT), kv_spec(n_ctx, V_HEAD)]
    args = [q, kc, vc]
    if has_seq:
        in_specs += [kv_spec(s, HEAD_SLOT), kv_spec(s, V_HEAD)]
        args += [k, v]
    return pl.pallas_call(
        functools.partial(_attn_kernel, has_seq=has_seq, sub=min(tq, 256)),
        grid=(nb, nh // hp, s // tq), in_specs=in_specs,
        out_specs=pl.BlockSpec((1, tq, hp * V_HEAD), lambda b, h, i: (b, i, h)),
        out_shape=jax.ShapeDtypeStruct((nb, s, nh * V_HEAD), BF16),
        compiler_params=_cparams(3), name="attention" if has_seq else "attention_ctx",
    )(*args)


def _seqdft_kernel(c_ref, s_ref, fab_ref, o_ref, *, scale):
    y = jnp.dot(c_ref[...], fab_ref[0, 0], preferred_element_type=F32)
    y = y + jnp.dot(s_ref[...], fab_ref[0, 1], preferred_element_type=F32)
    o_ref[0] = (y * scale).astype(BF16)


def _seqdft(dft_c, dft_s, fab, group_w):
    nb, _, s, fw = fab.shape
    tr = min(s, 1024)
    scale = 1.0 / math.sqrt(s * group_w)
    return pl.pallas_call(
        functools.partial(_seqdft_kernel, scale=scale),
        grid=(s // tr, nb),
        in_specs=[pl.BlockSpec((tr, s), lambda i, b: (i, 0)),
                  pl.BlockSpec((tr, s), lambda i, b: (i, 0)),
                  pl.BlockSpec((1, 2, s, fw), lambda i, b: (b, 0, 0, 0))],
        out_specs=pl.BlockSpec((1, tr, fw), lambda i, b: (b, i, 0)),
        out_shape=jax.ShapeDtypeStruct((nb, s, fw), BF16),
        compiler_params=_cparams(2), name="seqdft",
    )(dft_c, dft_s, fab)


HALO = 16
OUT_BLOCK = 512


def _mix_kernel(attn_ref, gb_ref, g_ref, gprev_ref, gnext_ref, fy_ref, x_ref, mod_ref, convw_ref,
                wout_ref, n2w_ref, wrcat_ref, wrhi_ref, br_ref, cntin_ref,
                x1_ref, h2_ref, route_ref, routet_ref, cnt_ref, carry_ref, hs0_ref, hs1_ref,
                *, tm, n_tiles, n_total, attn_w, conv_w):
    t = pl.program_id(0)
    i = t % n_tiles

    @pl.when(t == 0)
    def _():
        carry_ref[...] = cntin_ref[...]

    def mix_tile(hs_ref):
        state = {}

        def conv():
            g = g_ref[0].astype(F32)
            row = lax.broadcasted_iota(jnp.int32, g.shape, 0)
            prev_row = jnp.where(i > 0, gprev_ref[0, HALO - 1:HALO, :].astype(F32), 0.0)
            next_row = jnp.where(i < n_tiles - 1, gnext_ref[0, 0:1, :].astype(F32), 0.0)
            g_dn = jnp.where(row == 0, prev_row, pltpu.roll(g, 1, axis=0))
            g_up = jnp.where(row == tm - 1, next_row, pltpu.roll(g, tm - 1, axis=0))
            state["conv"] = (gb_ref[0].astype(F32) * (
                g_dn * convw_ref[0:1, :] + g * convw_ref[1:2, :] + g_up * convw_ref[2:3, :])).astype(BF16)

        def out_block(lo):
            def run():
                cs = slice(lo, lo + OUT_BLOCK)
                mix = jnp.dot(attn_ref[0], wout_ref[:attn_w, cs], preferred_element_type=F32)
                mix = mix + jnp.dot(state["conv"], wout_ref[attn_w:attn_w + conv_w, cs], preferred_element_type=F32)
                mix = mix + jnp.dot(fy_ref[0], wout_ref[attn_w + conv_w:, cs], preferred_element_type=F32)
                x1_ref[0, :, cs] = x_ref[0, :, cs] + mod_ref[0, 2:3, cs] * mix
            return run

        def norm2():
            h2 = _rms(x1_ref[0], n2w_ref[...]) * (1.0 + mod_ref[0, 4:5, :]) + mod_ref[0, 3:4, :]
            _pack_rows(h2_ref, h2)
            hs_ref[...] = h2

        return [conv] + [out_block(lo) for lo in range(0, x_ref.shape[2], OUT_BLOCK)] + [norm2]

    def route_tile(hs_ref):
        state = {}

        def logits():
            h2 = hs_ref[...]
            hi = h2.astype(BF16)
            lo = (h2 - hi.astype(F32)).astype(BF16)
            a = jnp.dot(hi, wrcat_ref[...], preferred_element_type=F32)
            state["logits"] = (a[:, :LANES] + a[:, LANES:] +
                               jnp.dot(lo, wrhi_ref[...], preferred_element_type=F32) + br_ref[...])

        def top():
            logits = state["logits"]
            lane = lax.broadcasted_iota(jnp.int32, logits.shape, 1)
            lane_f = lane.astype(F32)
            no_lane = float(LANES)
            is_group = lane < N_GROUPS
            gl = jnp.where(is_group, logits, NEG_BIG)
            gmax = jnp.max(gl, axis=-1, keepdims=True)
            gidx = jnp.min(jnp.where(gl == gmax, lane_f, no_lane), axis=-1, keepdims=True)
            g_p = 1.0 / jnp.sum(jnp.where(is_group, jnp.exp(gl - gmax), 0.0), axis=-1, keepdims=True)
            lane_group = ((lane - N_GROUPS) >> 3).astype(F32)
            in_group = (lane >= N_GROUPS) & (lane < N_GROUPS + N_EXPERTS) & (lane_group == gidx)
            el = jnp.where(in_group, logits, NEG_BIG)
            m1 = jnp.max(el, axis=-1, keepdims=True)
            i1 = jnp.min(jnp.where(el == m1, lane_f, no_lane), axis=-1, keepdims=True)
            el2 = jnp.where(lane_f == i1, NEG_BIG, el)
            m2 = jnp.max(el2, axis=-1, keepdims=True)
            i2 = jnp.min(jnp.where(el2 == m2, lane_f, no_lane), axis=-1, keepdims=True)
            e2 = jnp.exp(m2 - m1)
            state.update(lane=lane, lane_f=lane_f, i1=i1, i2=i2,
                         w1=g_p / (1.0 + e2), w2=g_p * e2 / (1.0 + e2))

        def ranks():
            lane, lane_f, i1, i2 = state["lane"], state["lane_f"], state["i1"], state["i2"]
            hit1, hit2 = lane_f == i1, lane_f == i2
            onehot = jnp.where(hit1 | hit2, 1.0, 0.0)
            ti = lax.broadcasted_iota(jnp.int32, (tm, tm), 0)
            tj = lax.broadcasted_iota(jnp.int32, (tm, tm), 1)
            before = jnp.where(tj < ti, 1.0, 0.0).astype(BF16)
            seen = jnp.dot(before, onehot.astype(BF16), preferred_element_type=F32) + carry_ref[...]
            r1 = jnp.sum(jnp.where(hit1, seen, 0.0), axis=-1, keepdims=True)
            r2 = jnp.sum(jnp.where(hit2, seen, 0.0), axis=-1, keepdims=True)
            carry_ref[...] = carry_ref[...] + jnp.sum(onehot, axis=0, keepdims=True)
            cnt_ref[...] = carry_ref[...]

            route = jnp.where(lane == 0, i1 - N_GROUPS, 0.0)
            route = jnp.where(lane == 1, i2 - N_GROUPS, route)
            route = jnp.where(lane == 2, r1, route)
            route = jnp.where(lane == 3, r2, route)
            route = jnp.where(lane == 4, state["w1"], route)
            route = jnp.where(lane == 5, state["w2"], route)
            route_ref[...] = route
            routet_ref[...] = route.T[:ROUTE_ROWS, :]

        return [logits, top, ranks]

    _two_stage(t, n_total, mix_tile, route_tile, (hs0_ref, hs1_ref))


def _mix(attn, gb, g, fy, x, mods, lw, cnt_in):
    nb, s, d = x.shape
    tm = min(s, 256)
    n_tiles = s // tm
    n_total = nb * n_tiles
    attn_w, conv_w = attn.shape[-1], gb.shape[-1]
    cur = lambda t: jnp.minimum(t, n_total - 1)
    old = lambda t: jnp.maximum(t - 1, 0)
    tok = lambda w: pl.BlockSpec((1, tm, w), lambda t: (cur(t) // n_tiles, cur(t) % n_tiles, 0))
    hb = tm // HALO
    in_specs = [tok(attn_w), tok(conv_w), tok(conv_w),
                pl.BlockSpec((1, HALO, conv_w),
                             lambda t: (cur(t) // n_tiles, jnp.maximum((cur(t) % n_tiles) * hb - 1, 0), 0)),
                pl.BlockSpec((1, HALO, conv_w),
                             lambda t: (cur(t) // n_tiles, jnp.minimum((cur(t) % n_tiles + 1) * hb, s // HALO - 1), 0)),
                tok(fy.shape[-1]), tok(d),
                pl.BlockSpec((1, 6, d), lambda t: (cur(t) // n_tiles, 0, 0))]
    weights = [lw["conv_w"], lw["w_out"], lw["n2w"], lw["wr_cat"], lw["wr_hi"], lw["b_r"], cnt_in]
    in_specs += [_const_spec(w.shape) for w in weights]
    n_tok = nb * s
    out_shape = [jax.ShapeDtypeStruct((nb, s, d), F32),
                 jax.ShapeDtypeStruct((n_tok * WORD_ROWS, LANES), U32),
                 jax.ShapeDtypeStruct((n_tok, LANES), F32),
                 jax.ShapeDtypeStruct((ROUTE_ROWS, n_tok), F32),
                 jax.ShapeDtypeStruct((1, LANES), F32)]
    out_specs = [tok(d),
                 pl.BlockSpec((tm * WORD_ROWS, LANES), lambda t: (cur(t), 0)),
                 pl.BlockSpec((tm, LANES), lambda t: (old(t), 0)),
                 pl.BlockSpec((ROUTE_ROWS, tm), lambda t: (0, old(t))),
                 pl.BlockSpec((1, LANES), lambda t: (0, 0))]
    return pl.pallas_call(
        functools.partial(_mix_kernel, tm=tm, n_tiles=n_tiles, n_total=n_total, attn_w=attn_w, conv_w=conv_w),
        grid=(n_total + 1,), in_specs=in_specs, out_specs=out_specs, out_shape=out_shape,
        scratch_shapes=[pltpu.VMEM((1, LANES), F32), pltpu.VMEM((tm, d), F32), pltpu.VMEM((tm, d), F32)],
        compiler_params=_cparams(1), name="mix_router",
    )(attn, gb, g, g, g, fy, x, mods, *weights)


def _row(ref, idx):
    return ref.at[pl.ds(pl.multiple_of(idx * WORD_ROWS, WORD_ROWS), WORD_ROWS), :]


DMA_UNROLL = 8
ZERO_ROWS = 128
WEIGHT_DMA_PRIORITY = 1


def _row_dma_loop(n_rows, copies, method, alternate=False):
    def body(jo, carry):
        n = 0
        for u in range(DMA_UNROLL):
            for c in copies(jo * DMA_UNROLL + u):
                if method == "start":
                    c.start(priority=n % 2 if alternate else 0)
                else:
                    c.wait()
                n += 1
        return carry
    lax.fori_loop(0, n_rows // DMA_UNROLL, body, 0)


def _tile_slots(slots_kt, tm):
    n_tok = slots_kt.shape[1]
    return slots_kt.reshape(TOP_K, n_tok // tm, tm).transpose(1, 0, 2).reshape(n_tok // tm, 1, TOP_K * tm)


def _dispatch_kernel(lo_ref, n_ref, slot_ref, *refs, tm, n_main_tiles, has_ctx):
    if has_ctx:
        h2_ref, h2c_ref, xs_ref, zero_ref, sem, zsem = refs
    else:
        h2_ref, xs_ref, zero_ref, sem, zsem = refs
    i = pl.program_id(0)

    @pl.when(i == 0)
    def _():
        zero_ref[...] = jnp.zeros(zero_ref.shape, U32)

        def zero_copy(first, rows):
            return pltpu.make_async_copy(
                zero_ref.at[pl.ds(0, rows * WORD_ROWS), :],
                xs_ref.at[pl.ds(pl.multiple_of(first * WORD_ROWS, WORD_ROWS), rows * WORD_ROWS), :], zsem)

        def fill(e, carry):
            lo, n = lo_ref[e], n_ref[e]
            n_full = n // ZERO_ROWS

            def run(method):
                def full(c, cc):
                    getattr(zero_copy(lo + c * ZERO_ROWS, ZERO_ROWS), method)()
                    return cc
                lax.fori_loop(0, n_full, full, 0)
                first = lo + n_full * ZERO_ROWS
                p = ZERO_ROWS // 2
                while p >= 1:
                    @pl.when((n & p) != 0)
                    def _(first=first, p=p):
                        getattr(zero_copy(first, p), method)()
                    first = first + (n & p)
                    p //= 2
            run("start")
            run("wait")
            return carry

        lax.fori_loop(0, lo_ref.shape[0], fill, 0)

    def scatter(src_ref):
        copies = lambda j: [
            pltpu.make_async_copy(_row(src_ref, j), _row(xs_ref, slot_ref[0, 0, k * tm + j]), sem)
            for k in range(TOP_K)]
        _row_dma_loop(tm, copies, "start", alternate=True)
        _row_dma_loop(tm, copies, "wait")

    if has_ctx:
        @pl.when(i < n_main_tiles)
        def _():
            scatter(h2_ref)

        @pl.when(i >= n_main_tiles)
        def _():
            scatter(h2c_ref)
    else:
        scatter(h2_ref)


def _dispatch(slots_kt, fill_lo, fill_n, h2, h2c, n_slots):
    tm = MOE_BLOCK
    n_main_tiles = h2.shape[0] // (tm * WORD_ROWS)
    has_ctx = h2c is not None
    slots3 = _tile_slots(slots_kt, tm)
    n_tiles = slots3.shape[0]
    blk = (tm * WORD_ROWS, LANES)
    in_specs = [pl.BlockSpec((1, 1, tm * TOP_K), lambda i, lo, n: (i, 0, 0), memory_space=pltpu.SMEM),
                pl.BlockSpec(blk, lambda i, lo, n: (jnp.minimum(i, n_main_tiles - 1), 0))]
    args = [slots3, h2]
    if has_ctx:
        in_specs.append(pl.BlockSpec(blk, lambda i, lo, n: (jnp.maximum(i - n_main_tiles, 0), 0)))
        args.append(h2c)
    grid_spec = pltpu.PrefetchScalarGridSpec(
        num_scalar_prefetch=2, grid=(n_tiles,), in_specs=in_specs,
        out_specs=pl.BlockSpec(memory_space=pl.ANY),
        scratch_shapes=[pltpu.VMEM((ZERO_ROWS * WORD_ROWS, LANES), U32),
                        pltpu.SemaphoreType.DMA(()), pltpu.SemaphoreType.DMA(())])
    return pl.pallas_call(
        functools.partial(_dispatch_kernel, tm=tm, n_main_tiles=n_main_tiles, has_ctx=has_ctx),
        grid_spec=grid_spec,
        out_shape=jax.ShapeDtypeStruct((n_slots * WORD_ROWS, LANES), U32),
        compiler_params=_cparams(1), name="dispatch",
    )(fill_lo, fill_n, *args)


def _expert_kernel(bexp_ref, nused_ref, nxt_ref, par_ref, xs_ref, wg_hbm, wu_hbm, wd_hbm, y_ref,
                   xb0_ref, xb1_ref, yb0_ref, yb1_ref, wgu_ref, wdb_ref, wgf_ref, wuf_ref, wdf_ref, wsem,
                   *, layer, d_expert, n_blocks):
    s = pl.program_id(0)
    n_used = nused_ref[0]
    half = xb0_ref.shape[1] // 2

    def unpack(xb_ref):
        def rows(r):
            def run():
                lo, hi = _unpack_pair(xs_ref[pl.ds(r, MOE_BLOCK, stride=WORD_ROWS), :])
                xb_ref[:, r * LANES:(r + 1) * LANES] = lo.astype(BF16)
                xb_ref[:, half + r * LANES:half + (r + 1) * LANES] = hi.astype(BF16)
            return run
        return [rows(r) for r in range(WORD_ROWS)]

    def mlp(xb_ref, yb_ref):
        state = {}

        def gate_up():
            gu = jnp.dot(xb_ref[...], wgu_ref[...], preferred_element_type=F32)
            gate, up = gu[:, :d_expert], gu[:, d_expert:]
            state["hid"] = (gate * jax.nn.sigmoid(gate) * up).astype(BF16)

        def down(lo):
            def run():
                yb_ref[:, lo:lo + half] = jnp.dot(state["hid"], wdb_ref[:, lo:lo + half], preferred_element_type=F32)
            return run

        return [gate_up, down(0), down(half)]

    def store(yb_ref):
        def rows(r):
            def run():
                y_ref[pl.ds(r, MOE_BLOCK, stride=WORD_ROWS), :] = _pack_pair(
                    yb_ref[:, r * LANES:(r + 1) * LANES], yb_ref[:, half + r * LANES:half + (r + 1) * LANES])
            return run
        return [rows(r) for r in range(WORD_ROWS)]

    def weight_copies(e, buf):
        return [pltpu.make_async_copy(src.at[layer, e], dst.at[buf], wsem.at[buf])
                for src, dst in ((wg_hbm, wgf_ref), (wu_hbm, wuf_ref), (wd_hbm, wdf_ref))]

    run_unpack = s < n_used
    run_mlp = (s >= 1) & (s <= n_used)
    run_store = (s >= 2) & (s <= n_used + 1)
    steady = (s >= 2) & (s < n_used)

    mlp_block = jnp.clip(s - 1, 0, n_blocks - 1)
    new_expert = (s == 1) | (bexp_ref[mlp_block] != bexp_ref[jnp.maximum(mlp_block - 1, 0)])

    @pl.when(s == 0)
    def _():
        for c in weight_copies(bexp_ref[0], 0):
            c.start(priority=WEIGHT_DMA_PRIORITY)

    @pl.when(run_mlp & new_expert)
    def _():
        e = bexp_ref[mlp_block]
        buf = par_ref[e]
        for c in weight_copies(e, buf):
            c.wait()
        nxt = nxt_ref[e]

        @pl.when(nxt < N_EXPERTS)
        def _():
            for c in weight_copies(nxt, 1 - buf):
                c.start(priority=WEIGHT_DMA_PRIORITY)

        wgu_ref[:, :d_expert] = wgf_ref[buf].astype(BF16)
        wgu_ref[:, d_expert:] = wuf_ref[buf].astype(BF16)
        wdb_ref[...] = wdf_ref[buf].astype(BF16)

    for parity, (xb_new, xb_cur, yb_cur, yb_old) in enumerate(
            [(xb0_ref, xb1_ref, yb1_ref, yb0_ref), (xb1_ref, xb0_ref, yb0_ref, yb1_ref)]):
        mine = (s % 2) == parity

        @pl.when(steady & mine)
        def _():
            _interleave(mlp(xb_cur, yb_cur), unpack(xb_new), store(yb_old))

        @pl.when(jnp.logical_not(steady) & mine)
        def _():
            @pl.when(run_unpack)
            def _():
                _interleave(unpack(xb_new))

            @pl.when(run_mlp)
            def _():
                _interleave(mlp(xb_cur, yb_cur))

            @pl.when(run_store)
            def _():
                _interleave(store(yb_old))

    @pl.when(s >= n_used + 2)
    def _():
        y_ref[...] = jnp.zeros(y_ref.shape, U32)


def _experts(layer, block_exp, n_used, next_exp, run_par, xs, w_gate, w_up, w_down):
    _, _, d, d_expert = w_gate.shape
    n_blocks = block_exp.shape[0]
    blk = (MOE_BLOCK * WORD_ROWS, LANES)
    hbm = pl.BlockSpec(memory_space=pl.ANY)
    grid_spec = pltpu.PrefetchScalarGridSpec(
        num_scalar_prefetch=4, grid=(n_blocks + 2,),
        in_specs=[pl.BlockSpec(blk, lambda s, be, nu, nx, pr: (jnp.minimum(s, nu[0] - 1), 0)), hbm, hbm, hbm],
        out_specs=pl.BlockSpec(blk, lambda s, be, nu, nx, pr: (jnp.maximum(s - 2, 0), 0)),
        scratch_shapes=[pltpu.VMEM((MOE_BLOCK, d), BF16), pltpu.VMEM((MOE_BLOCK, d), BF16),
                        pltpu.VMEM((MOE_BLOCK, d), F32), pltpu.VMEM((MOE_BLOCK, d), F32),
                        pltpu.VMEM((d, 2 * d_expert), BF16),
                        pltpu.VMEM((d_expert, d), BF16),
                        pltpu.VMEM((2, d, d_expert), F32), pltpu.VMEM((2, d, d_expert), F32),
                        pltpu.VMEM((2, d_expert, d), F32),
                        pltpu.SemaphoreType.DMA((2,))])
    return pl.pallas_call(
        functools.partial(_expert_kernel, layer=layer, d_expert=d_expert, n_blocks=n_blocks),
        grid_spec=grid_spec,
        out_shape=jax.ShapeDtypeStruct(xs.shape, U32),
        compiler_params=_cparams(1), name="experts",
    )(block_exp, n_used, next_exp, run_par, xs, w_gate, w_up, w_down)


def _combine_kernel(slot_ref, slotn_ref, x1_ref, mod_ref, route_ref, y_ref, o_ref, ybuf_ref, sem, *, tm, n_total):
    t = pl.program_id(0)
    half_rows = TOP_K * tm
    cur = t % 2

    def gathers(slots, buf):
        return lambda j: [
            pltpu.make_async_copy(_row(y_ref, slots[0, 0, k * tm + j]), _row(ybuf_ref, buf * half_rows + k * tm + j),
                                  sem.at[buf])
            for k in range(TOP_K)]

    @pl.when(t == 0)
    def _():
        _row_dma_loop(tm, gathers(slot_ref, 0), "start", alternate=True)

    @pl.when(t + 1 < n_total)
    def _():
        _row_dma_loop(tm, gathers(slotn_ref, 1 - cur), "start", alternate=True)

    _row_dma_loop(tm, gathers(slot_ref, cur), "wait")

    w1, w2 = route_ref[:, 4:5], route_ref[:, 5:6]
    base = cur * half_rows * WORD_ROWS
    half = x1_ref.shape[2] // 2
    for r in range(WORD_ROWS):
        a_lo, a_hi = _unpack_pair(ybuf_ref[pl.ds(base + r, tm, stride=WORD_ROWS), :])
        b_lo, b_hi = _unpack_pair(ybuf_ref[pl.ds(base + tm * WORD_ROWS + r, tm, stride=WORD_ROWS), :])
        for off, ya, yb in ((0, a_lo, b_lo), (half, a_hi, b_hi)):
            lanes = slice(off + r * LANES, off + (r + 1) * LANES)
            o_ref[0, :, lanes] = x1_ref[0, :, lanes] + mod_ref[0, 5:6, lanes] * (ya * w1 + yb * w2)


def _combine(slots_kt, x1, mods, route, y):
    nb, s, d = x1.shape
    tm = min(s, MOE_BLOCK)
    n_tiles = s // tm
    n_total = nb * n_tiles
    slots3 = _tile_slots(slots_kt, tm)
    slot_spec = lambda nxt: pl.BlockSpec((1, 1, tm * TOP_K), lambda t: (jnp.minimum(t + nxt, n_total - 1), 0, 0),
                                         memory_space=pltpu.SMEM)
    return pl.pallas_call(
        functools.partial(_combine_kernel, tm=tm, n_total=n_total),
        grid=(n_total,),
        in_specs=[slot_spec(0), slot_spec(1),
                  pl.BlockSpec((1, tm, d), lambda t: (t // n_tiles, t % n_tiles, 0)),
                  pl.BlockSpec((1, 6, d), lambda t: (t // n_tiles, 0, 0)),
                  pl.BlockSpec((tm, LANES), lambda t: (t, 0)),
                  pl.BlockSpec(memory_space=pl.ANY)],
        out_specs=pl.BlockSpec((1, tm, d), lambda t: (t // n_tiles, t % n_tiles, 0)),
        out_shape=jax.ShapeDtypeStruct((nb, s, d), F32),
        scratch_shapes=[pltpu.VMEM((2 * TOP_K * tm * WORD_ROWS, LANES), U32), pltpu.SemaphoreType.DMA((2,))],
        compiler_params=_cparams(1), name="combine",
    )(slots3, slots3, x1, mods, route, y)


def _layer_weights(l, d, norm1_w, norm2_w, w_in, q_a_norm_w, w_uq, kv_a_norm_w, w_ukv, q_norm_w, k_norm_w,
                   conv_w, w_out, w_rg, b_rg, w_re, b_re):
    q_lora, kv_lora = q_a_norm_w.shape[1], kv_a_norm_w.shape[1]
    conv_cols = conv_w.shape[2]
    in_cols = w_in.shape[2]
    four_cols = in_cols - (q_lora + kv_lora + QK_ROPE + 3 * conv_cols)
    o_ckv = q_lora
    o_kr = o_ckv + kv_lora
    o_gb = o_kr + QK_ROPE
    o_gc, o_u, o_f = o_gb + conv_cols, o_gb + 2 * conv_cols, o_gb + 3 * conv_cols
    wi = w_in[l]
    w_in_p = jnp.concatenate([wi[:, :o_kr], wi[:, o_gb:], wi[:, o_kr:o_gb],
                              jnp.zeros((d, LANES - QK_ROPE), F32)], axis=1).astype(BF16)
    names, widths = ["cq", "ckv", "gb", "gc", "u", "f", "kr"], [q_lora, kv_lora, conv_cols, conv_cols, conv_cols, four_cols, LANES]
    cols, o = {}, 0
    for n, w in zip(names, widths):
        cols[n] = (o, o + w)
        o += w
    pad_head = lambda w: jnp.pad(w, [(0, 0)] * (w.ndim - 1) + [(0, HEAD_SLOT - QK_HEAD)])
    w_uq_p = pad_head(w_uq[l].reshape(q_lora, N_HEADS, QK_HEAD)).reshape(q_lora, N_HEADS * HEAD_SLOT).astype(BF16)
    w_ukv_p = (w_ukv[l].reshape(kv_lora, N_HEADS, 2, QK_NOPE).transpose(0, 2, 1, 3)
               .reshape(kv_lora, 2 * N_HEADS * QK_NOPE).astype(BF16))
    w_r = jnp.concatenate([w_rg[l], w_re[l], jnp.zeros((d, LANES - N_GROUPS - N_EXPERTS), F32)], axis=1)
    wr_hi = w_r.astype(BF16)
    wr_lo = (w_r - wr_hi.astype(F32)).astype(BF16)
    b_r = jnp.concatenate([b_rg[l], b_re[l], jnp.zeros((LANES - N_GROUPS - N_EXPERTS,), F32)]).reshape(1, LANES)
    return dict(cols=cols, n1w=norm1_w[l].reshape(1, d), n2w=norm2_w[l].reshape(1, d), w_in=w_in_p,
                qanw=q_a_norm_w[l].reshape(1, q_lora), w_uq=w_uq_p, kvanw=kv_a_norm_w[l].reshape(1, kv_lora),
                w_ukv=w_ukv_p, qnw=pad_head(q_norm_w[l]).reshape(1, HEAD_SLOT),
                knw=pad_head(k_norm_w[l]).reshape(1, HEAD_SLOT), conv_w=conv_w[l], w_out=w_out[l].astype(BF16),
                wr_cat=jnp.concatenate([wr_hi, wr_lo], axis=1), wr_hi=wr_hi, b_r=b_r)


def _rope_tables(n_tok):
    freqs = QK_ROPE // 4
    pos = jnp.arange(n_tok)
    row = (pos // GRID_W).astype(F32)
    colp = (pos % GRID_W).astype(F32)
    inv = ROPE_THETA ** (-jnp.arange(freqs, dtype=F32) / freqs)
    ar, ac = row[:, None] * inv, colp[:, None] * inv
    zeros = jnp.zeros((n_tok, LANES - QK_ROPE), F32)
    rope_c = jnp.concatenate([jnp.cos(ar), jnp.cos(ar), jnp.cos(ac), jnp.cos(ac), zeros], axis=1)
    rope_s = jnp.concatenate([-jnp.sin(ar), jnp.sin(ar), -jnp.sin(ac), jnp.sin(ac), zeros], axis=1)
    return rope_c, rope_s


def _identity_rope(n_tok):
    ones = jnp.concatenate([jnp.ones((n_tok, QK_ROPE), F32), jnp.zeros((n_tok, LANES - QK_ROPE), F32)], axis=1)
    return ones, jnp.zeros((n_tok, LANES), F32)


def _dft_tables(n):
    idx = jnp.arange(n, dtype=jnp.int32)
    ang = ((idx[:, None] * idx[None, :]) % n).astype(F32) * (2.0 * math.pi / n)
    return jnp.cos(ang), jnp.sin(ang)


def _slot_plan(route_t, counts):
    n_tok = route_t.shape[1]
    expert = route_t[0:TOP_K].astype(jnp.int32)
    rank = route_t[TOP_K:2 * TOP_K].astype(jnp.int32)
    padded = (counts + MOE_BLOCK - 1) // MOE_BLOCK * MOE_BLOCK
    pad_ends = jnp.cumsum(padded)
    pad_starts = pad_ends - padded
    base = functools.reduce(lambda acc, e: jnp.where(expert == e, pad_starts[e], acc), range(N_EXPERTS),
                            jnp.zeros_like(expert))
    slots = base + rank
    n_blocks = -(-(n_tok * TOP_K) // MOE_BLOCK) + N_EXPERTS
    n_slots = n_blocks * MOE_BLOCK
    block_start = jnp.arange(n_blocks, dtype=jnp.int32) * MOE_BLOCK
    block_exp = jnp.sum((pad_ends[None, :] <= block_start[:, None]).astype(jnp.int32), axis=1)
    block_exp = jnp.minimum(block_exp, N_EXPERTS - 1)
    n_used = (pad_ends[-1] // MOE_BLOCK).astype(jnp.int32)
    last_exp = block_exp[jnp.maximum(n_used - 1, 0)]
    block_exp = jnp.where(jnp.arange(n_blocks) < n_used, block_exp, last_exp)
    fill_lo = jnp.concatenate([pad_starts + counts, pad_ends[-1:]]).astype(jnp.int32)
    fill_n = jnp.concatenate([padded - counts, n_slots - pad_ends[-1:]]).astype(jnp.int32)
    ids = jnp.arange(N_EXPERTS, dtype=jnp.int32)
    live_id = jnp.where(counts > 0, ids, N_EXPERTS)
    live_from = lax.cummin(live_id[::-1])[::-1]
    next_exp = jnp.concatenate([live_from[1:], jnp.full((1,), N_EXPERTS, jnp.int32)])
    run_par = (jnp.cumsum(counts > 0) - (counts > 0)).astype(jnp.int32) & 1
    return (slots.astype(jnp.int32), block_exp, n_used.reshape(1), next_exp.astype(jnp.int32), run_par,
            fill_lo, fill_n, n_slots)


def kernel(x, c, ctx, c_ctx, w_mod, b_mod, norm1_w, norm2_w, w_in, q_a_norm_w, w_uq, kv_a_norm_w, w_ukv, q_norm_w, k_norm_w, conv_w, w_out, w_router_group, b_router_group, w_router_expert, b_router_expert, w_gate, w_up, w_down):
    nb, s, d = x.shape
    n_ctx = ctx.shape[1]
    depth = w_mod.shape[0]
    group_w = conv_w.shape[2] // FOURIER_GROUPS

    rows = -(-(nb + 1) // 8) * 8
    c_all = jnp.concatenate([c, c_ctx[None, :], jnp.zeros((rows - nb - 1, d), F32)], axis=0)
    mod_all = _modulation(c_all, w_mod, b_mod)

    rope_c, rope_s = _rope_tables(s)
    id_c, id_s = _identity_rope(n_ctx)
    cc, sc = _dft_tables(group_w)
    wdft = jnp.concatenate([cc, -sc], axis=1).astype(BF16)
    seq_c, seq_s = (t.astype(BF16) for t in _dft_tables(s))
    ctx_c, ctx_s = (t.astype(BF16) for t in _dft_tables(n_ctx))

    xc = ctx
    for l in range(depth):
        last = l == depth - 1
        lw = _layer_weights(l, d, norm1_w, norm2_w, w_in, q_a_norm_w, w_uq, kv_a_norm_w, w_ukv, q_norm_w,
                            k_norm_w, conv_w, w_out, w_router_group, b_router_group, w_router_expert,
                            b_router_expert)
        mods = mod_all[l, :nb].reshape(nb, 6, d)
        mods_c = jnp.broadcast_to(mod_all[l, nb].reshape(1, 6, d), (nb, 6, d))

        q, k, v, gb, g, fab = _inproj(x, mods, lw, rope_c, rope_s, wdft, kv_only=False)
        if last:
            kc, vc = _inproj(xc, mods_c, lw, id_c, id_s, wdft, kv_only=True)
        else:
            qc, kc, vc, gbc, gc, fabc = _inproj(xc, mods_c, lw, id_c, id_s, wdft, kv_only=False)
        attn = _attention(q, kc, vc, k, v)
        fy = _seqdft(seq_c, seq_s, fab, group_w)
        zero_cnt = jnp.zeros((1, LANES), F32)
        x1, h2, route, route_t, cnt = _mix(attn, gb, g, fy, x, mods, lw, zero_cnt)
        if not last:
            attn_c = _attention(qc, kc, vc)
            fyc = _seqdft(ctx_c, ctx_s, fabc, group_w)
            x1c, h2c, route_c, route_tc, cnt = _mix(attn_c, gbc, gc, fyc, xc, mods_c, lw, cnt)
            route_t = jnp.concatenate([route_t, route_tc], axis=1)
        else:
            h2c = None

        counts = cnt[0, N_GROUPS:N_GROUPS + N_EXPERTS].astype(jnp.int32)
        slots, block_exp, n_used, next_exp, run_par, fill_lo, fill_n, n_slots = _slot_plan(route_t, counts)
        xs = _dispatch(slots, fill_lo, fill_n, h2, h2c, n_slots)
        y = _experts(l, block_exp, n_used, next_exp, run_par, xs, w_gate, w_up, w_down)
        n_main = nb * s
        x = _combine(slots[:, :n_main], x1, mods, route, y)
        if not last:
            xc = _combine(slots[:, n_main:], x1c, mods_c, route_c, y)
    return x
```

```python
import functools
import math

import jax
import jax.numpy as jnp
from jax import lax
from jax.experimental import pallas as pl
from jax.experimental.pallas import tpu as pltpu

F32 = jnp.float32
BF16 = jnp.bfloat16

EPS = 1e-6
N_HEADS = 8
QK_NOPE = 128
QK_ROPE = 64
QK_HEAD = QK_NOPE + QK_ROPE
V_HEAD = 128
GRID_W = 64
ROPE_THETA = 10000.0
ATTN_SCALE = QK_HEAD ** -0.5
Q_SCALE = ATTN_SCALE * math.log2(math.e)
CONV_K = 3
FOURIER_GROUPS = 4
N_GROUPS = 8
EXPERTS_PER_GROUP = 8
N_EXPERTS = N_GROUPS * EXPERTS_PER_GROUP
TOP_K = 2
MOE_BLOCK = 256

LANES = 128
HEAD_SLOT = 2 * LANES
WORD_ROWS = 8
ROUTE_ROWS = 8
U32 = jnp.uint32
PROJ_BLOCK = 512
VMEM_LIMIT = 56 * 1024 * 1024
NEG_BIG = -1e30


def _cparams(n_axes):
    return pltpu.CompilerParams(dimension_semantics=("arbitrary",) * n_axes,
                                vmem_limit_bytes=VMEM_LIMIT)


def _pack_pair(lo, hi):
    lo_bits = lax.bitcast_convert_type(lo.astype(BF16).astype(F32), U32) >> 16
    hi_bits = lax.bitcast_convert_type(hi.astype(BF16).astype(F32), U32) & U32(0xFFFF0000)
    return hi_bits | lo_bits


def _unpack_pair(words):
    lo = lax.bitcast_convert_type(words << 16, F32)
    hi = lax.bitcast_convert_type(words & U32(0xFFFF0000), F32)
    return lo, hi


def _pack_rows(dst_ref, val):
    rows, d = val.shape
    half = d // 2
    for r in range(WORD_ROWS):
        dst_ref[pl.ds(r, rows, stride=WORD_ROWS), :] = _pack_pair(
            val[:, r * LANES:(r + 1) * LANES], val[:, half + r * LANES:half + (r + 1) * LANES])


def _const_spec(shape):
    nd = len(shape)
    return pl.BlockSpec(shape, lambda *_: (0,) * nd, pipeline_mode=pl.Buffered(1))


def _mod_kernel(c_ref, w_ref, b_ref, o_ref):
    c = c_ref[...]
    a = (c * jax.nn.sigmoid(c)).astype(BF16)
    o_ref[0] = jnp.dot(a, w_ref[0].astype(BF16), preferred_element_type=F32) + b_ref[0]


def _modulation(c_all, w_mod, b_mod):
    n_layers, d, n_out = w_mod.shape
    rows = c_all.shape[0]
    tn = 1024
    return pl.pallas_call(
        _mod_kernel,
        grid=(n_layers, n_out // tn),
        in_specs=[pl.BlockSpec((rows, d), lambda l, j: (0, 0)),
                  pl.BlockSpec((1, d, tn), lambda l, j: (l, 0, j)),
                  pl.BlockSpec((1, 1, tn), lambda l, j: (l, 0, j))],
        out_specs=pl.BlockSpec((1, rows, tn), lambda l, j: (l, 0, j)),
        out_shape=jax.ShapeDtypeStruct((n_layers, rows, n_out), F32),
        compiler_params=_cparams(2),
        name="modulation",
    )(c_all, w_mod, b_mod.reshape(n_layers, 1, n_out))


def _rms(t, w):
    return t * lax.rsqrt(jnp.mean(t * t, axis=-1, keepdims=True) + EPS) * w


def _rope_partner(t):
    lane = lax.broadcasted_iota(jnp.int32, t.shape, 1)
    ahead = pltpu.roll(t, LANES - 16, axis=1)
    behind = pltpu.roll(t, 16, axis=1)
    return jnp.where((lane & 16) == 0, ahead, behind)


def _inproj_kernel(x_ref, mod_ref, n1w_ref, win_ref, qanw_ref, wuq_ref, kvanw_ref, wukv_ref,
                   qnw_ref, knw_ref, rc_ref, rs_ref, wdft_ref, *rest, cols, pcols, kv_only, n_total):
    if kv_only:
        k_ref, v_ref, p0_ref, p1_ref = rest
    else:
        q_ref, k_ref, v_ref, gb_ref, g_ref, fab_ref, p0_ref, p1_ref, dft_ref = rest
    t = pl.program_id(0)

    def project(p_ref):
        state = {}

        def norm():
            shift, scale = mod_ref[0, 0:1, :], mod_ref[0, 1:2, :]
            state["hb"] = (_rms(x_ref[0], n1w_ref[...]) * (1.0 + scale) + shift).astype(BF16)

        def block(src, dst):
            def run():
                p_ref[:, dst[0]:dst[1]] = jnp.dot(state["hb"], win_ref[:, src[0]:src[1]],
                                                   preferred_element_type=F32)
            return run

        if kv_only:
            return [norm] + [block(cols[n], pcols[n]) for n in pcols]
        width = win_ref.shape[1]
        return [norm] + [block((lo, min(lo + PROJ_BLOCK, width)), (lo, min(lo + PROJ_BLOCK, width)))
                         for lo in range(0, width, PROJ_BLOCK)]

    def finish(p_ref):
        col = lambda n: p_ref[:, pcols[n][0]:pcols[n][1]]
        state = {}

        def rope(v):
            return v * rc_ref[...] + _rope_partner(v) * rs_ref[...]

        def kv_prep():
            ckv = _rms(col("ckv"), kvanw_ref[...]).astype(BF16)
            state["kv"] = jnp.dot(ckv, wukv_ref[...], preferred_element_type=F32)
            k_r = col("kr")
            state["ss_r"] = jnp.sum(k_r * k_r, axis=-1, keepdims=True)
            state["kr_rot"] = rope(k_r * knw_ref[:, LANES:])

        def k_head(h):
            def run():
                kv, v_off = state["kv"], N_HEADS * QK_NOPE
                kn = kv[:, h * QK_NOPE:(h + 1) * QK_NOPE]
                r = lax.rsqrt((jnp.sum(kn * kn, axis=-1, keepdims=True) + state["ss_r"]) * (1.0 / QK_HEAD) + EPS)
                k_ref[0, h, :, :LANES] = (kn * r * knw_ref[:, :LANES]).astype(BF16)
                k_ref[0, h, :, LANES:] = (state["kr_rot"] * r).astype(BF16)
                v_ref[0, h] = kv[:, v_off + h * V_HEAD:v_off + (h + 1) * V_HEAD].astype(BF16)
            return run

        steps = [kv_prep] + [k_head(h) for h in range(N_HEADS)]
        if kv_only:
            return steps

        def q_prep():
            cq = _rms(col("cq"), qanw_ref[...]).astype(BF16)
            state["qraw"] = jnp.dot(cq, wuq_ref[...], preferred_element_type=F32)

        def q_head(h):
            def run():
                qraw = state["qraw"]
                q0 = qraw[:, h * HEAD_SLOT:h * HEAD_SLOT + LANES]
                q1 = qraw[:, h * HEAD_SLOT + LANES:(h + 1) * HEAD_SLOT]
                ss = jnp.sum(q0 * q0, axis=-1, keepdims=True) + jnp.sum(q1 * q1, axis=-1, keepdims=True)
                r = lax.rsqrt(ss * (1.0 / QK_HEAD) + EPS) * Q_SCALE
                q_ref[0, h, :, :LANES] = (q0 * r * qnw_ref[:, :LANES]).astype(BF16)
                q_ref[0, h, :, LANES:] = rope(q1 * r * qnw_ref[:, LANES:]).astype(BF16)
            return run

        def conv_gates():
            gb_ref[0] = col("gb").astype(BF16)
            g_ref[0] = (col("gc") * col("u")).astype(BF16)

        def dft_group(g):
            def run():
                f = p_ref[:, pcols["f"][0] + g * LANES:pcols["f"][0] + (g + 1) * LANES].astype(BF16)
                res = jnp.dot(f, wdft_ref[...], preferred_element_type=F32)
                half_rows = res.shape[0] // 2
                for part in (0, 1):
                    dft_ref[2 * g + part] = res[:, part * LANES:(part + 1) * LANES]
                    for parity in (0, 1):
                        fab_ref[0, part, parity, :, g * LANES:(g + 1) * LANES] = (
                            dft_ref[2 * g + part, pl.ds(parity, half_rows, stride=2), :].astype(BF16))
            return run

        return (steps + [q_prep] + [q_head(h) for h in range(N_HEADS)] + [conv_gates] +
                [dft_group(g) for g in range(FOURIER_GROUPS)])

    _two_stage(t, n_total, project, finish, (p0_ref, p1_ref))


def _interleave(*streams):
    total = max(len(st) for st in streams)
    done = [0] * len(streams)
    for step in range(1, total + 1):
        for k, st in enumerate(streams):
            upto = -(-step * len(st) // total)
            while done[k] < upto:
                st[done[k]]()
                done[k] += 1


def _two_stage(t, n_total, first, second, bufs):
    steady = (t >= 1) & (t < n_total)
    for parity in (0, 1):
        mine = (t % 2) == parity
        new_buf, old_buf = bufs[parity], bufs[1 - parity]

        @pl.when(steady & mine)
        def _():
            _interleave(first(new_buf), second(old_buf))

        @pl.when((t == 0) & mine)
        def _():
            _interleave(first(new_buf))

        @pl.when((t == n_total) & mine)
        def _():
            _interleave(second(old_buf))


def _inproj(x, mods, lw, rope_c, rope_s, wdft, *, kv_only):
    nb, s, d = x.shape
    tm = min(s, 256)
    n_tiles = s // tm
    n_total = nb * n_tiles
    cols = lw["cols"]
    if kv_only:
        w_ckv = cols["ckv"][1] - cols["ckv"][0]
        pcols = {"ckv": (0, w_ckv), "kr": (w_ckv, w_ckv + LANES)}
    else:
        pcols = cols
    p_width = max(hi for _, hi in pcols.values())
    fw = wdft.shape[0] * FOURIER_GROUPS
    cw = cols["gb"][1] - cols["gb"][0]
    kernel = functools.partial(_inproj_kernel, cols=cols, pcols=pcols, kv_only=kv_only, n_total=n_total)
    cur = lambda t: jnp.minimum(t, n_total - 1)
    old = lambda t: jnp.maximum(t - 1, 0)
    head_spec = lambda w: pl.BlockSpec((1, N_HEADS, tm, w), lambda t: (old(t) // n_tiles, 0, old(t) % n_tiles, 0))
    tok_spec = lambda w: pl.BlockSpec((1, tm, w), lambda t: (old(t) // n_tiles, old(t) % n_tiles, 0))
    out_shape = [jax.ShapeDtypeStruct((nb, N_HEADS, s, HEAD_SLOT), BF16),
                 jax.ShapeDtypeStruct((nb, N_HEADS, s, V_HEAD), BF16)]
    out_specs = [head_spec(HEAD_SLOT), head_spec(V_HEAD)]
    if not kv_only:
        out_shape = ([jax.ShapeDtypeStruct((nb, N_HEADS, s, HEAD_SLOT), BF16)] + out_shape +
                     [jax.ShapeDtypeStruct((nb, s, cw), BF16), jax.ShapeDtypeStruct((nb, s, cw), BF16),
                      jax.ShapeDtypeStruct((nb, 2, 2, s // 2, fw), BF16)])
        out_specs = ([head_spec(HEAD_SLOT)] + out_specs +
                     [tok_spec(cw), tok_spec(cw),
                      pl.BlockSpec((1, 2, 2, tm // 2, fw),
                                   lambda t: (old(t) // n_tiles, 0, 0, old(t) % n_tiles, 0))])
    weights = [lw["n1w"], lw["w_in"], lw["qanw"], lw["w_uq"], lw["kvanw"], lw["w_ukv"], lw["qnw"], lw["knw"]]
    in_specs = ([pl.BlockSpec((1, tm, d), lambda t: (cur(t) // n_tiles, cur(t) % n_tiles, 0)),
                 pl.BlockSpec((1, 6, d), lambda t: (cur(t) // n_tiles, 0, 0))] +
                [_const_spec(w.shape) for w in weights] +
                [pl.BlockSpec((tm, LANES), lambda t: (old(t) % n_tiles, 0)),
                 pl.BlockSpec((tm, LANES), lambda t: (old(t) % n_tiles, 0)),
                 _const_spec(wdft.shape)])
    return pl.pallas_call(
        kernel, grid=(n_total + 1,), in_specs=in_specs, out_specs=out_specs, out_shape=out_shape,
        scratch_shapes=([pltpu.VMEM((tm, p_width), F32), pltpu.VMEM((tm, p_width), F32)] +
                        ([] if kv_only else [pltpu.VMEM((2 * FOURIER_GROUPS, tm, LANES), F32)])),
        compiler_params=_cparams(1), name="inproj_kv" if kv_only else "inproj",
    )(x, mods, lw["n1w"], lw["w_in"], lw["qanw"], lw["w_uq"], lw["kvanw"], lw["w_ukv"], lw["qnw"],
      lw["knw"], rope_c, rope_s, wdft)


_NT = (((1,), (1,)), ((), ()))
ATTN_HEADS_PER_STEP = 2


def _attn_kernel(q_ref, kc_ref, vc_ref, *rest, has_seq, sub):
    o_ref = rest[-1]
    key_refs = [kc_ref] + ([rest[0]] if has_seq else [])
    val_refs = [vc_ref] + ([rest[1]] if has_seq else [])

    def scores(h, r0):
        q = q_ref[0, h, r0:r0 + sub, :]
        return [lax.dot_general(q, k[0, h], _NT, preferred_element_type=F32) for k in key_refs]

    def finish(h, r0, s_parts):
        m = functools.reduce(jnp.maximum, [jnp.max(s, axis=-1, keepdims=True) for s in s_parts])
        p_parts = [jnp.exp2(s - m) for s in s_parts]
        l = sum(jnp.sum(p, axis=-1, keepdims=True) for p in p_parts)
        o = sum(jnp.dot(p.astype(BF16), v[0, h], preferred_element_type=F32) for p, v in zip(p_parts, val_refs))
        o_ref[0, r0:r0 + sub, h * V_HEAD:(h + 1) * V_HEAD] = (o / l).astype(BF16)

    items = [(h, r0) for h in range(q_ref.shape[1]) for r0 in range(0, q_ref.shape[2], sub)]
    pending = scores(*items[0])
    for n, item in enumerate(items):
        upcoming = scores(*items[n + 1]) if n + 1 < len(items) else None
        finish(*item, pending)
        pending = upcoming


def _attention(q, kc, vc, k=None, v=None):
    nb, nh, s, _ = q.shape
    n_ctx = kc.shape[2]
    has_seq = k is not None
    tq = min(s, 2048)
    hp = ATTN_HEADS_PER_STEP if has_seq else nh
    kv_spec = lambda n, w: pl.BlockSpec((1, hp, n, w), lambda b, h, i: (b, h, 0, 0))
    in_specs = [pl.BlockSpec((1, hp, tq, HEAD_SLOT), lambda b, h, i: (b, h, i, 0)),
                kv_spec(n_ctx, HEAD_SLOT), kv_spec(n_ctx, V_HEAD)]
    args = [q, kc, vc]
    if has_seq:
        in_specs += [kv_spec(s, HEAD_SLOT), kv_spec(s, V_HEAD)]
        args += [k, v]
    return pl.pallas_call(
        functools.partial(_attn_kernel, has_seq=has_seq, sub=min(tq, 256)),
        grid=(nb, nh // hp, s // tq), in_specs=in_specs,
        out_specs=pl.BlockSpec((1, tq, hp * V_HEAD), lambda b, h, i: (b, i, h)),
        out_shape=jax.ShapeDtypeStruct((nb, s, nh * V_HEAD), BF16),
        compiler_params=_cparams(3), name="attention" if has_seq else "attention_ctx",
    )(*args)


def _seqdft_kernel(ce_ref, se_ref, co_ref, so_ref, fab_ref, o_ref, *, scale):
    dot = functools.partial(jnp.dot, preferred_element_type=F32)
    even = dot(ce_ref[...], fab_ref[0, 0, 0]) + dot(se_ref[...], fab_ref[0, 1, 0])
    odd = dot(co_ref[...], fab_ref[0, 0, 1]) + dot(so_ref[...], fab_ref[0, 1, 1])
    o_ref[0, 0] = ((even + odd) * scale).astype(BF16)
    o_ref[0, 1] = ((even - odd) * scale).astype(BF16)


def _seqdft(tables, fab, group_w):
    nb, _, _, half, fw = fab.shape
    s = 2 * half
    tr = min(half, 512)
    scale = 1.0 / math.sqrt(s * group_w)
    tab = pl.BlockSpec((tr, half), lambda i, b: (i, 0))
    out = pl.pallas_call(
        functools.partial(_seqdft_kernel, scale=scale),
        grid=(half // tr, nb),
        in_specs=[tab, tab, tab, tab,
                  pl.BlockSpec((1, 2, 2, half, fw), lambda i, b: (b, 0, 0, 0, 0))],
        out_specs=pl.BlockSpec((1, 2, tr, fw), lambda i, b: (b, 0, i, 0)),
        out_shape=jax.ShapeDtypeStruct((nb, 2, half, fw), BF16),
        compiler_params=_cparams(2), name="seqdft",
    )(*tables, fab)
    return out.reshape(nb, s, fw)


HALO = 16
OUT_BLOCK = 512


def _mix_kernel(attn_ref, gb_ref, g_ref, gprev_ref, gnext_ref, fy_ref, x_ref, mod_ref, convw_ref,
                wout_ref, n2w_ref, wrcat_ref, wrhi_ref, br_ref, cntin_ref,
                x1_ref, h2_ref, route_ref, routet_ref, cnt_ref, carry_ref, hs0_ref, hs1_ref,
                *, tm, n_tiles, n_total, attn_w, conv_w):
    t = pl.program_id(0)
    i = t % n_tiles

    @pl.when(t == 0)
    def _():
        carry_ref[...] = cntin_ref[...]

    def mix_tile(hs_ref):
        state = {}

        def conv():
            g = g_ref[0].astype(F32)
            row = lax.broadcasted_iota(jnp.int32, g.shape, 0)
            prev_row = jnp.where(i > 0, gprev_ref[0, HALO - 1:HALO, :].astype(F32), 0.0)
            next_row = jnp.where(i < n_tiles - 1, gnext_ref[0, 0:1, :].astype(F32), 0.0)
            g_dn = jnp.where(row == 0, prev_row, pltpu.roll(g, 1, axis=0))
            g_up = jnp.where(row == tm - 1, next_row, pltpu.roll(g, tm - 1, axis=0))
            state["conv"] = (gb_ref[0].astype(F32) * (
                g_dn * convw_ref[0:1, :] + g * convw_ref[1:2, :] + g_up * convw_ref[2:3, :])).astype(BF16)

        def out_block(lo):
            def run():
                cs = slice(lo, lo + OUT_BLOCK)
                mix = jnp.dot(attn_ref[0], wout_ref[:attn_w, cs], preferred_element_type=F32)
                mix = mix + jnp.dot(state["conv"], wout_ref[attn_w:attn_w + conv_w, cs], preferred_element_type=F32)
                mix = mix + jnp.dot(fy_ref[0], wout_ref[attn_w + conv_w:, cs], preferred_element_type=F32)
                x1_ref[0, :, cs] = x_ref[0, :, cs] + mod_ref[0, 2:3, cs] * mix
            return run

        def norm2():
            h2 = _rms(x1_ref[0], n2w_ref[...]) * (1.0 + mod_ref[0, 4:5, :]) + mod_ref[0, 3:4, :]
            _pack_rows(h2_ref, h2)
            hs_ref[...] = h2

        return [conv] + [out_block(lo) for lo in range(0, x_ref.shape[2], OUT_BLOCK)] + [norm2]

    def route_tile(hs_ref):
        state = {}

        def logits():
            h2 = hs_ref[...]
            hi = h2.astype(BF16)
            lo = (h2 - hi.astype(F32)).astype(BF16)
            a = jnp.dot(hi, wrcat_ref[...], preferred_element_type=F32)
            state["logits"] = (a[:, :LANES] + a[:, LANES:] +
                               jnp.dot(lo, wrhi_ref[...], preferred_element_type=F32) + br_ref[...])

        def top():
            logits = state["logits"]
            lane = lax.broadcasted_iota(jnp.int32, logits.shape, 1)
            lane_f = lane.astype(F32)
            no_lane = float(LANES)
            is_group = lane < N_GROUPS
            gl = jnp.where(is_group, logits, NEG_BIG)
            gmax = jnp.max(gl, axis=-1, keepdims=True)
            gidx = jnp.min(jnp.where(gl == gmax, lane_f, no_lane), axis=-1, keepdims=True)
            g_p = 1.0 / jnp.sum(jnp.where(is_group, jnp.exp(gl - gmax), 0.0), axis=-1, keepdims=True)
            lane_group = ((lane - N_GROUPS) >> 3).astype(F32)
            in_group = (lane >= N_GROUPS) & (lane < N_GROUPS + N_EXPERTS) & (lane_group == gidx)
            el = jnp.where(in_group, logits, NEG_BIG)
            m1 = jnp.max(el, axis=-1, keepdims=True)
            i1 = jnp.min(jnp.where(el == m1, lane_f, no_lane), axis=-1, keepdims=True)
            el2 = jnp.where(lane_f == i1, NEG_BIG, el)
            m2 = jnp.max(el2, axis=-1, keepdims=True)
            i2 = jnp.min(jnp.where(el2 == m2, lane_f, no_lane), axis=-1, keepdims=True)
            e2 = jnp.exp(m2 - m1)
            state.update(lane=lane, lane_f=lane_f, i1=i1, i2=i2,
                         w1=g_p / (1.0 + e2), w2=g_p * e2 / (1.0 + e2))

        def ranks():
            lane, lane_f, i1, i2 = state["lane"], state["lane_f"], state["i1"], state["i2"]
            hit1, hit2 = lane_f == i1, lane_f == i2
            onehot = jnp.where(hit1 | hit2, 1.0, 0.0)
            ti = lax.broadcasted_iota(jnp.int32, (tm, tm), 0)
            tj = lax.broadcasted_iota(jnp.int32, (tm, tm), 1)
            before = jnp.where(tj < ti, 1.0, 0.0).astype(BF16)
            seen = jnp.dot(before, onehot.astype(BF16), preferred_element_type=F32) + carry_ref[...]
            r1 = jnp.sum(jnp.where(hit1, seen, 0.0), axis=-1, keepdims=True)
            r2 = jnp.sum(jnp.where(hit2, seen, 0.0), axis=-1, keepdims=True)
            carry_ref[...] = carry_ref[...] + jnp.sum(onehot, axis=0, keepdims=True)
            cnt_ref[...] = carry_ref[...]

            route = jnp.where(lane == 0, i1 - N_GROUPS, 0.0)
            route = jnp.where(lane == 1, i2 - N_GROUPS, route)
            route = jnp.where(lane == 2, r1, route)
            route = jnp.where(lane == 3, r2, route)
            route = jnp.where(lane == 4, state["w1"], route)
            route = jnp.where(lane == 5, state["w2"], route)
            route_ref[...] = route
            routet_ref[...] = route.T[:ROUTE_ROWS, :]

        return [logits, top, ranks]

    _two_stage(t, n_total, mix_tile, route_tile, (hs0_ref, hs1_ref))


def _mix(attn, gb, g, fy, x, mods, lw, cnt_in):
    nb, s, d = x.shape
    tm = min(s, 256)
    n_tiles = s // tm
    n_total = nb * n_tiles
    attn_w, conv_w = attn.shape[-1], gb.shape[-1]
    cur = lambda t: jnp.minimum(t, n_total - 1)
    old = lambda t: jnp.maximum(t - 1, 0)
    tok = lambda w: pl.BlockSpec((1, tm, w), lambda t: (cur(t) // n_tiles, cur(t) % n_tiles, 0))
    hb = tm // HALO
    in_specs = [tok(attn_w), tok(conv_w), tok(conv_w),
                pl.BlockSpec((1, HALO, conv_w),
                             lambda t: (cur(t) // n_tiles, jnp.maximum((cur(t) % n_tiles) * hb - 1, 0), 0)),
                pl.BlockSpec((1, HALO, conv_w),
                             lambda t: (cur(t) // n_tiles, jnp.minimum((cur(t) % n_tiles + 1) * hb, s // HALO - 1), 0)),
                tok(fy.shape[-1]), tok(d),
                pl.BlockSpec((1, 6, d), lambda t: (cur(t) // n_tiles, 0, 0))]
    weights = [lw["conv_w"], lw["w_out"], lw["n2w"], lw["wr_cat"], lw["wr_hi"], lw["b_r"], cnt_in]
    in_specs += [_const_spec(w.shape) for w in weights]
    n_tok = nb * s
    out_shape = [jax.ShapeDtypeStruct((nb, s, d), F32),
                 jax.ShapeDtypeStruct((n_tok * WORD_ROWS, LANES), U32),
                 jax.ShapeDtypeStruct((n_tok, LANES), F32),
                 jax.ShapeDtypeStruct((ROUTE_ROWS, n_tok), F32),
                 jax.ShapeDtypeStruct((1, LANES), F32)]
    out_specs = [tok(d),
                 pl.BlockSpec((tm * WORD_ROWS, LANES), lambda t: (cur(t), 0)),
                 pl.BlockSpec((tm, LANES), lambda t: (old(t), 0)),
                 pl.BlockSpec((ROUTE_ROWS, tm), lambda t: (0, old(t))),
                 pl.BlockSpec((1, LANES), lambda t: (0, 0))]
    return pl.pallas_call(
        functools.partial(_mix_kernel, tm=tm, n_tiles=n_tiles, n_total=n_total, attn_w=attn_w, conv_w=conv_w),
        grid=(n_total + 1,), in_specs=in_specs, out_specs=out_specs, out_shape=out_shape,
        scratch_shapes=[pltpu.VMEM((1, LANES), F32), pltpu.VMEM((tm, d), F32), pltpu.VMEM((tm, d), F32)],
        compiler_params=_cparams(1), name="mix_router",
    )(attn, gb, g, g, g, fy, x, mods, *weights)


def _row(ref, idx):
    return ref.at[pl.ds(pl.multiple_of(idx * WORD_ROWS, WORD_ROWS), WORD_ROWS), :]


DMA_UNROLL = 8
ZERO_ROWS = 128
WEIGHT_DMA_PRIORITY = 1


def _row_dma_loop(n_rows, copies, method, alternate=False):
    def body(jo, carry):
        n = 0
        for u in range(DMA_UNROLL):
            for c in copies(jo * DMA_UNROLL + u):
                if method == "start":
                    c.start(priority=n % 2 if alternate else 0)
                else:
                    c.wait()
                n += 1
        return carry
    lax.fori_loop(0, n_rows // DMA_UNROLL, body, 0)


def _tile_slots(slots_kt, tm):
    n_tok = slots_kt.shape[1]
    return slots_kt.reshape(TOP_K, n_tok // tm, tm).transpose(1, 0, 2).reshape(n_tok // tm, 1, TOP_K * tm)


def _dispatch_kernel(lo_ref, n_ref, slot_ref, *refs, tm, n_main_tiles, has_ctx):
    if has_ctx:
        h2_ref, h2c_ref, xs_ref, zero_ref, sem, zsem = refs
    else:
        h2_ref, xs_ref, zero_ref, sem, zsem = refs
    i = pl.program_id(0)

    @pl.when(i == 0)
    def _():
        zero_ref[...] = jnp.zeros(zero_ref.shape, U32)

        def zero_copy(first, rows):
            return pltpu.make_async_copy(
                zero_ref.at[pl.ds(0, rows * WORD_ROWS), :],
                xs_ref.at[pl.ds(pl.multiple_of(first * WORD_ROWS, WORD_ROWS), rows * WORD_ROWS), :], zsem)

        def fill(e, carry):
            lo, n = lo_ref[e], n_ref[e]
            n_full = n // ZERO_ROWS

            def run(method):
                def full(c, cc):
                    getattr(zero_copy(lo + c * ZERO_ROWS, ZERO_ROWS), method)()
                    return cc
                lax.fori_loop(0, n_full, full, 0)
                first = lo + n_full * ZERO_ROWS
                p = ZERO_ROWS // 2
                while p >= 1:
                    @pl.when((n & p) != 0)
                    def _(first=first, p=p):
                        getattr(zero_copy(first, p), method)()
                    first = first + (n & p)
                    p //= 2
            run("start")
            run("wait")
            return carry

        lax.fori_loop(0, lo_ref.shape[0], fill, 0)

    def scatter(src_ref):
        copies = lambda j: [
            pltpu.make_async_copy(_row(src_ref, j), _row(xs_ref, slot_ref[0, 0, k * tm + j]), sem)
            for k in range(TOP_K)]
        _row_dma_loop(tm, copies, "start", alternate=True)
        _row_dma_loop(tm, copies, "wait")

    if has_ctx:
        @pl.when(i < n_main_tiles)
        def _():
            scatter(h2_ref)

        @pl.when(i >= n_main_tiles)
        def _():
            scatter(h2c_ref)
    else:
        scatter(h2_ref)


def _dispatch(slots_kt, fill_lo, fill_n, h2, h2c, n_slots):
    tm = MOE_BLOCK
    n_main_tiles = h2.shape[0] // (tm * WORD_ROWS)
    has_ctx = h2c is not None
    slots3 = _tile_slots(slots_kt, tm)
    n_tiles = slots3.shape[0]
    blk = (tm * WORD_ROWS, LANES)
    in_specs = [pl.BlockSpec((1, 1, tm * TOP_K), lambda i, lo, n: (i, 0, 0), memory_space=pltpu.SMEM),
                pl.BlockSpec(blk, lambda i, lo, n: (jnp.minimum(i, n_main_tiles - 1), 0))]
    args = [slots3, h2]
    if has_ctx:
        in_specs.append(pl.BlockSpec(blk, lambda i, lo, n: (jnp.maximum(i - n_main_tiles, 0), 0)))
        args.append(h2c)
    grid_spec = pltpu.PrefetchScalarGridSpec(
        num_scalar_prefetch=2, grid=(n_tiles,), in_specs=in_specs,
        out_specs=pl.BlockSpec(memory_space=pl.ANY),
        scratch_shapes=[pltpu.VMEM((ZERO_ROWS * WORD_ROWS, LANES), U32),
                        pltpu.SemaphoreType.DMA(()), pltpu.SemaphoreType.DMA(())])
    return pl.pallas_call(
        functools.partial(_dispatch_kernel, tm=tm, n_main_tiles=n_main_tiles, has_ctx=has_ctx),
        grid_spec=grid_spec,
        out_shape=jax.ShapeDtypeStruct((n_slots * WORD_ROWS, LANES), U32),
        compiler_params=_cparams(1), name="dispatch",
    )(fill_lo, fill_n, *args)


def _expert_kernel(bexp_ref, nused_ref, nxt_ref, par_ref, xs_ref, wg_hbm, wu_hbm, wd_hbm, y_ref,
                   xb0_ref, xb1_ref, yb0_ref, yb1_ref, wgu_ref, wdb_ref, wgf_ref, wuf_ref, wdf_ref, wsem,
                   *, layer, d_expert, n_blocks):
    s = pl.program_id(0)
    n_used = nused_ref[0]
    half = xb0_ref.shape[1] // 2

    def unpack(xb_ref):
        def rows(r):
            def run():
                lo, hi = _unpack_pair(xs_ref[pl.ds(r, MOE_BLOCK, stride=WORD_ROWS), :])
                xb_ref[:, r * LANES:(r + 1) * LANES] = lo.astype(BF16)
                xb_ref[:, half + r * LANES:half + (r + 1) * LANES] = hi.astype(BF16)
            return run
        return [rows(r) for r in range(WORD_ROWS)]

    def mlp(xb_ref, yb_ref):
        state = {}

        def gate_up():
            gu = jnp.dot(xb_ref[...], wgu_ref[...], preferred_element_type=F32)
            gate, up = gu[:, :d_expert], gu[:, d_expert:]
            state["hid"] = (gate * jax.nn.sigmoid(gate) * up).astype(BF16)

        def down(lo):
            def run():
                yb_ref[:, lo:lo + half] = jnp.dot(state["hid"], wdb_ref[:, lo:lo + half], preferred_element_type=F32)
            return run

        return [gate_up, down(0), down(half)]

    def store(yb_ref):
        def rows(r):
            def run():
                y_ref[pl.ds(r, MOE_BLOCK, stride=WORD_ROWS), :] = _pack_pair(
                    yb_ref[:, r * LANES:(r + 1) * LANES], yb_ref[:, half + r * LANES:half + (r + 1) * LANES])
            return run
        return [rows(r) for r in range(WORD_ROWS)]

    def weight_copies(e, buf):
        return [pltpu.make_async_copy(src.at[layer, e], dst.at[buf], wsem.at[buf])
                for src, dst in ((wg_hbm, wgf_ref), (wu_hbm, wuf_ref), (wd_hbm, wdf_ref))]

    run_unpack = s < n_used
    run_mlp = (s >= 1) & (s <= n_used)
    run_store = (s >= 2) & (s <= n_used + 1)
    steady = (s >= 2) & (s < n_used)

    mlp_block = jnp.clip(s - 1, 0, n_blocks - 1)
    new_expert = (s == 1) | (bexp_ref[mlp_block] != bexp_ref[jnp.maximum(mlp_block - 1, 0)])

    @pl.when(s == 0)
    def _():
        for c in weight_copies(bexp_ref[0], 0):
            c.start(priority=WEIGHT_DMA_PRIORITY)

    @pl.when(run_mlp & new_expert)
    def _():
        e = bexp_ref[mlp_block]
        buf = par_ref[e]
        for c in weight_copies(e, buf):
            c.wait()
        nxt = nxt_ref[e]

        @pl.when(nxt < N_EXPERTS)
        def _():
            for c in weight_copies(nxt, 1 - buf):
                c.start(priority=WEIGHT_DMA_PRIORITY)

        wgu_ref[:, :d_expert] = wgf_ref[buf].astype(BF16)
        wgu_ref[:, d_expert:] = wuf_ref[buf].astype(BF16)
        wdb_ref[...] = wdf_ref[buf].astype(BF16)

    for parity, (xb_new, xb_cur, yb_cur, yb_old) in enumerate(
            [(xb0_ref, xb1_ref, yb1_ref, yb0_ref), (xb1_ref, xb0_ref, yb0_ref, yb1_ref)]):
        mine = (s % 2) == parity

        @pl.when(steady & mine)
        def _():
            _interleave(mlp(xb_cur, yb_cur), unpack(xb_new), store(yb_old))

        @pl.when(jnp.logical_not(steady) & mine)
        def _():
            @pl.when(run_unpack)
            def _():
                _interleave(unpack(xb_new))

            @pl.when(run_mlp)
            def _():
                _interleave(mlp(xb_cur, yb_cur))

            @pl.when(run_store)
            def _():
                _interleave(store(yb_old))

    @pl.when(s >= n_used + 2)
    def _():
        y_ref[...] = jnp.zeros(y_ref.shape, U32)


def _experts(layer, block_exp, n_used, next_exp, run_par, xs, w_gate, w_up, w_down):
    _, _, d, d_expert = w_gate.shape
    n_blocks = block_exp.shape[0]
    blk = (MOE_BLOCK * WORD_ROWS, LANES)
    hbm = pl.BlockSpec(memory_space=pl.ANY)
    grid_spec = pltpu.PrefetchScalarGridSpec(
        num_scalar_prefetch=4, grid=(n_blocks + 2,),
        in_specs=[pl.BlockSpec(blk, lambda s, be, nu, nx, pr: (jnp.minimum(s, nu[0] - 1), 0)), hbm, hbm, hbm],
        out_specs=pl.BlockSpec(blk, lambda s, be, nu, nx, pr: (jnp.maximum(s - 2, 0), 0)),
        scratch_shapes=[pltpu.VMEM((MOE_BLOCK, d), BF16), pltpu.VMEM((MOE_BLOCK, d), BF16),
                        pltpu.VMEM((MOE_BLOCK, d), F32), pltpu.VMEM((MOE_BLOCK, d), F32),
                        pltpu.VMEM((d, 2 * d_expert), BF16),
                        pltpu.VMEM((d_expert, d), BF16),
                        pltpu.VMEM((2, d, d_expert), F32), pltpu.VMEM((2, d, d_expert), F32),
                        pltpu.VMEM((2, d_expert, d), F32),
                        pltpu.SemaphoreType.DMA((2,))])
    return pl.pallas_call(
        functools.partial(_expert_kernel, layer=layer, d_expert=d_expert, n_blocks=n_blocks),
        grid_spec=grid_spec,
        out_shape=jax.ShapeDtypeStruct(xs.shape, U32),
        compiler_params=_cparams(1), name="experts",
    )(block_exp, n_used, next_exp, run_par, xs, w_gate, w_up, w_down)


def _combine_kernel(slot_ref, slotn_ref, x1_ref, mod_ref, route_ref, y_ref, o_ref, ybuf_ref, sem, *, tm, n_total):
    t = pl.program_id(0)
    half_rows = TOP_K * tm
    cur = t % 2

    def gathers(slots, buf):
        return lambda j: [
            pltpu.make_async_copy(_row(y_ref, slots[0, 0, k * tm + j]), _row(ybuf_ref, buf * half_rows + k * tm + j),
                                  sem.at[buf])
            for k in range(TOP_K)]

    @pl.when(t == 0)
    def _():
        _row_dma_loop(tm, gathers(slot_ref, 0), "start", alternate=True)

    @pl.when(t + 1 < n_total)
    def _():
        _row_dma_loop(tm, gathers(slotn_ref, 1 - cur), "start", alternate=True)

    _row_dma_loop(tm, gathers(slot_ref, cur), "wait")

    w1, w2 = route_ref[:, 4:5], route_ref[:, 5:6]
    base = cur * half_rows * WORD_ROWS
    half = x1_ref.shape[2] // 2
    for r in range(WORD_ROWS):
        a_lo, a_hi = _unpack_pair(ybuf_ref[pl.ds(base + r, tm, stride=WORD_ROWS), :])
        b_lo, b_hi = _unpack_pair(ybuf_ref[pl.ds(base + tm * WORD_ROWS + r, tm, stride=WORD_ROWS), :])
        for off, ya, yb in ((0, a_lo, b_lo), (half, a_hi, b_hi)):
            lanes = slice(off + r * LANES, off + (r + 1) * LANES)
            o_ref[0, :, lanes] = x1_ref[0, :, lanes] + mod_ref[0, 5:6, lanes] * (ya * w1 + yb * w2)


def _combine(slots_kt, x1, mods, route, y):
    nb, s, d = x1.shape
    tm = min(s, MOE_BLOCK)
    n_tiles = s // tm
    n_total = nb * n_tiles
    slots3 = _tile_slots(slots_kt, tm)
    slot_spec = lambda nxt: pl.BlockSpec((1, 1, tm * TOP_K), lambda t: (jnp.minimum(t + nxt, n_total - 1), 0, 0),
                                         memory_space=pltpu.SMEM)
    return pl.pallas_call(
        functools.partial(_combine_kernel, tm=tm, n_total=n_total),
        grid=(n_total,),
        in_specs=[slot_spec(0), slot_spec(1),
                  pl.BlockSpec((1, tm, d), lambda t: (t // n_tiles, t % n_tiles, 0)),
                  pl.BlockSpec((1, 6, d), lambda t: (t // n_tiles, 0, 0)),
                  pl.BlockSpec((tm, LANES), lambda t: (t, 0)),
                  pl.BlockSpec(memory_space=pl.ANY)],
        out_specs=pl.BlockSpec((1, tm, d), lambda t: (t // n_tiles, t % n_tiles, 0)),
        out_shape=jax.ShapeDtypeStruct((nb, s, d), F32),
        scratch_shapes=[pltpu.VMEM((2 * TOP_K * tm * WORD_ROWS, LANES), U32), pltpu.SemaphoreType.DMA((2,))],
        compiler_params=_cparams(1), name="combine",
    )(slots3, slots3, x1, mods, route, y)


def _layer_weights(l, d, norm1_w, norm2_w, w_in, q_a_norm_w, w_uq, kv_a_norm_w, w_ukv, q_norm_w, k_norm_w,
                   conv_w, w_out, w_rg, b_rg, w_re, b_re):
    q_lora, kv_lora = q_a_norm_w.shape[1], kv_a_norm_w.shape[1]
    conv_cols = conv_w.shape[2]
    in_cols = w_in.shape[2]
    four_cols = in_cols - (q_lora + kv_lora + QK_ROPE + 3 * conv_cols)
    o_ckv = q_lora
    o_kr = o_ckv + kv_lora
    o_gb = o_kr + QK_ROPE
    o_gc, o_u, o_f = o_gb + conv_cols, o_gb + 2 * conv_cols, o_gb + 3 * conv_cols
    wi = w_in[l]
    w_in_p = jnp.concatenate([wi[:, :o_kr], wi[:, o_gb:], wi[:, o_kr:o_gb],
                              jnp.zeros((d, LANES - QK_ROPE), F32)], axis=1).astype(BF16)
    names, widths = ["cq", "ckv", "gb", "gc", "u", "f", "kr"], [q_lora, kv_lora, conv_cols, conv_cols, conv_cols, four_cols, LANES]
    cols, o = {}, 0
    for n, w in zip(names, widths):
        cols[n] = (o, o + w)
        o += w
    pad_head = lambda w: jnp.pad(w, [(0, 0)] * (w.ndim - 1) + [(0, HEAD_SLOT - QK_HEAD)])
    w_uq_p = pad_head(w_uq[l].reshape(q_lora, N_HEADS, QK_HEAD)).reshape(q_lora, N_HEADS * HEAD_SLOT).astype(BF16)
    w_ukv_p = (w_ukv[l].reshape(kv_lora, N_HEADS, 2, QK_NOPE).transpose(0, 2, 1, 3)
               .reshape(kv_lora, 2 * N_HEADS * QK_NOPE).astype(BF16))
    w_r = jnp.concatenate([w_rg[l], w_re[l], jnp.zeros((d, LANES - N_GROUPS - N_EXPERTS), F32)], axis=1)
    wr_hi = w_r.astype(BF16)
    wr_lo = (w_r - wr_hi.astype(F32)).astype(BF16)
    b_r = jnp.concatenate([b_rg[l], b_re[l], jnp.zeros((LANES - N_GROUPS - N_EXPERTS,), F32)]).reshape(1, LANES)
    return dict(cols=cols, n1w=norm1_w[l].reshape(1, d), n2w=norm2_w[l].reshape(1, d), w_in=w_in_p,
                qanw=q_a_norm_w[l].reshape(1, q_lora), w_uq=w_uq_p, kvanw=kv_a_norm_w[l].reshape(1, kv_lora),
                w_ukv=w_ukv_p, qnw=pad_head(q_norm_w[l]).reshape(1, HEAD_SLOT),
                knw=pad_head(k_norm_w[l]).reshape(1, HEAD_SLOT), conv_w=conv_w[l], w_out=w_out[l].astype(BF16),
                wr_cat=jnp.concatenate([wr_hi, wr_lo], axis=1), wr_hi=wr_hi, b_r=b_r)


def _rope_tables(n_tok):
    freqs = QK_ROPE // 4
    pos = jnp.arange(n_tok)
    row = (pos // GRID_W).astype(F32)
    colp = (pos % GRID_W).astype(F32)
    inv = ROPE_THETA ** (-jnp.arange(freqs, dtype=F32) / freqs)
    ar, ac = row[:, None] * inv, colp[:, None] * inv
    zeros = jnp.zeros((n_tok, LANES - QK_ROPE), F32)
    rope_c = jnp.concatenate([jnp.cos(ar), jnp.cos(ar), jnp.cos(ac), jnp.cos(ac), zeros], axis=1)
    rope_s = jnp.concatenate([-jnp.sin(ar), jnp.sin(ar), -jnp.sin(ac), jnp.sin(ac), zeros], axis=1)
    return rope_c, rope_s


def _identity_rope(n_tok):
    ones = jnp.concatenate([jnp.ones((n_tok, QK_ROPE), F32), jnp.zeros((n_tok, LANES - QK_ROPE), F32)], axis=1)
    return ones, jnp.zeros((n_tok, LANES), F32)


def _dft_tables(n):
    idx = jnp.arange(n, dtype=jnp.int32)
    ang = ((idx[:, None] * idx[None, :]) % n).astype(F32) * (2.0 * math.pi / n)
    return jnp.cos(ang), jnp.sin(ang)


def _split_dft_tables(n):
    half = n // 2
    j = jnp.arange(half, dtype=jnp.int32)[:, None]
    m = jnp.arange(half, dtype=jnp.int32)[None, :]
    even = ((j * 2 * m) % n).astype(F32) * (2.0 * math.pi / n)
    odd = ((j * (2 * m + 1)) % n).astype(F32) * (2.0 * math.pi / n)
    return tuple(t.astype(BF16) for t in (jnp.cos(even), jnp.sin(even), jnp.cos(odd), jnp.sin(odd)))


def _slot_plan(route_t, counts):
    n_tok = route_t.shape[1]
    expert = route_t[0:TOP_K].astype(jnp.int32).reshape(-1)
    rank = route_t[TOP_K:2 * TOP_K].astype(jnp.int32).reshape(-1)
    padded = (counts + MOE_BLOCK - 1) // MOE_BLOCK * MOE_BLOCK
    pad_ends = jnp.cumsum(padded)
    pad_starts = pad_ends - padded
    base = functools.reduce(lambda acc, e: jnp.where(expert == e, pad_starts[e], acc), range(N_EXPERTS),
                            jnp.zeros_like(expert))
    slots = (base + rank).reshape(TOP_K, n_tok)
    n_blocks = -(-(n_tok * TOP_K) // MOE_BLOCK) + N_EXPERTS
    n_slots = n_blocks * MOE_BLOCK
    block_start = jnp.arange(n_blocks, dtype=jnp.int32) * MOE_BLOCK
    block_exp = jnp.sum((pad_ends[None, :] <= block_start[:, None]).astype(jnp.int32), axis=1)
    block_exp = jnp.minimum(block_exp, N_EXPERTS - 1)
    n_used = (pad_ends[-1] // MOE_BLOCK).astype(jnp.int32)
    last_exp = block_exp[jnp.maximum(n_used - 1, 0)]
    block_exp = jnp.where(jnp.arange(n_blocks) < n_used, block_exp, last_exp)
    fill_lo = jnp.concatenate([pad_starts + counts, pad_ends[-1:]]).astype(jnp.int32)
    fill_n = jnp.concatenate([padded - counts, n_slots - pad_ends[-1:]]).astype(jnp.int32)
    ids = jnp.arange(N_EXPERTS, dtype=jnp.int32)
    live_id = jnp.where(counts > 0, ids, N_EXPERTS)
    live_from = lax.cummin(live_id[::-1])[::-1]
    next_exp = jnp.concatenate([live_from[1:], jnp.full((1,), N_EXPERTS, jnp.int32)])
    run_par = (jnp.cumsum(counts > 0) - (counts > 0)).astype(jnp.int32) & 1
    return (slots.astype(jnp.int32), block_exp, n_used.reshape(1), next_exp.astype(jnp.int32), run_par,
            fill_lo, fill_n, n_slots)


def kernel(x, c, ctx, c_ctx, w_mod, b_mod, norm1_w, norm2_w, w_in, q_a_norm_w, w_uq, kv_a_norm_w, w_ukv, q_norm_w, k_norm_w, conv_w, w_out, w_router_group, b_router_group, w_router_expert, b_router_expert, w_gate, w_up, w_down):
    nb, s, d = x.shape
    n_ctx = ctx.shape[1]
    depth = w_mod.shape[0]
    group_w = conv_w.shape[2] // FOURIER_GROUPS

    rows = -(-(nb + 1) // 8) * 8
    c_all = jnp.concatenate([c, c_ctx[None, :], jnp.zeros((rows - nb - 1, d), F32)], axis=0)
    mod_all = _modulation(c_all, w_mod, b_mod)

    rope_c, rope_s = _rope_tables(s)
    id_c, id_s = _identity_rope(n_ctx)
    cc, sc = _dft_tables(group_w)
    wdft = jnp.concatenate([cc, -sc], axis=1).astype(BF16)
    seq_tables = _split_dft_tables(s)
    ctx_tables = _split_dft_tables(n_ctx)

    xc = ctx
    for l in range(depth):
        last = l == depth - 1
        lw = _layer_weights(l, d, norm1_w, norm2_w, w_in, q_a_norm_w, w_uq, kv_a_norm_w, w_ukv, q_norm_w,
                            k_norm_w, conv_w, w_out, w_router_group, b_router_group, w_router_expert,
                            b_router_expert)
        mods = mod_all[l, :nb].reshape(nb, 6, d)
        mods_c = jnp.broadcast_to(mod_all[l, nb].reshape(1, 6, d), (nb, 6, d))

        q, k, v, gb, g, fab = _inproj(x, mods, lw, rope_c, rope_s, wdft, kv_only=False)
        if last:
            kc, vc = _inproj(xc, mods_c, lw, id_c, id_s, wdft, kv_only=True)
        else:
            qc, kc, vc, gbc, gc, fabc = _inproj(xc, mods_c, lw, id_c, id_s, wdft, kv_only=False)
        attn = _attention(q, kc, vc, k, v)
        fy = _seqdft(seq_tables, fab, group_w)
        zero_cnt = jnp.zeros((1, LANES), F32)
        x1, h2, route, route_t, cnt = _mix(attn, gb, g, fy, x, mods, lw, zero_cnt)
        if not last:
            attn_c = _attention(qc, kc, vc)
            fyc = _seqdft(ctx_tables, fabc, group_w)
            x1c, h2c, route_c, route_tc, cnt = _mix(attn_c, gbc, gc, fyc, xc, mods_c, lw, cnt)
            route_t = jnp.concatenate([route_t, route_tc], axis=1)
        else:
            h2c = None

        counts = cnt[0, N_GROUPS:N_GROUPS + N_EXPERTS].astype(jnp.int32)
        slots, block_exp, n_used, next_exp, run_par, fill_lo, fill_n, n_slots = _slot_plan(route_t, counts)
        xs = _dispatch(slots, fill_lo, fill_n, h2, h2c, n_slots)
        y = _experts(l, block_exp, n_used, next_exp, run_par, xs, w_gate, w_up, w_down)
        n_main = nb * s
        x = _combine(slots[:, :n_main], x1, mods, route, y)
        if not last:
            xc = _combine(slots[:, n_main:], x1c, mods_c, route_c, y)
    return x
```

```python
import functools
import math

import jax
import jax.numpy as jnp
from jax import lax
from jax.experimental import pallas as pl
from jax.experimental.pallas import tpu as pltpu

F32 = jnp.float32
BF16 = jnp.bfloat16

EPS = 1e-6
N_HEADS = 8
QK_NOPE = 128
QK_ROPE = 64
QK_HEAD = QK_NOPE + QK_ROPE
V_HEAD = 128
GRID_W = 64
ROPE_THETA = 10000.0
ATTN_SCALE = QK_HEAD ** -0.5
Q_SCALE = ATTN_SCALE * math.log2(math.e)
CONV_K = 3
FOURIER_GROUPS = 4
N_GROUPS = 8
EXPERTS_PER_GROUP = 8
N_EXPERTS = N_GROUPS * EXPERTS_PER_GROUP
TOP_K = 2
MOE_BLOCK = 256

LANES = 128
HEAD_SLOT = 2 * LANES
WORD_ROWS = 8
ROUTE_ROWS = 8
U32 = jnp.uint32
PROJ_BLOCK = 512
VMEM_LIMIT = 56 * 1024 * 1024
NEG_BIG = -1e30


def _cparams(n_axes):
    return pltpu.CompilerParams(dimension_semantics=("arbitrary",) * n_axes,
                                vmem_limit_bytes=VMEM_LIMIT)


def _pack_pair(lo, hi):
    lo_bits = lax.bitcast_convert_type(lo.astype(BF16).astype(F32), U32) >> 16
    hi_bits = lax.bitcast_convert_type(hi.astype(BF16).astype(F32), U32) & U32(0xFFFF0000)
    return hi_bits | lo_bits


def _unpack_pair(words):
    lo = lax.bitcast_convert_type(words << 16, F32)
    hi = lax.bitcast_convert_type(words & U32(0xFFFF0000), F32)
    return lo, hi


def _pack_rows(dst_ref, val):
    rows, d = val.shape
    half = d // 2
    for r in range(WORD_ROWS):
        dst_ref[pl.ds(r, rows, stride=WORD_ROWS), :] = _pack_pair(
            val[:, r * LANES:(r + 1) * LANES], val[:, half + r * LANES:half + (r + 1) * LANES])


def _const_spec(shape):
    nd = len(shape)
    return pl.BlockSpec(shape, lambda *_: (0,) * nd, pipeline_mode=pl.Buffered(1))


def _mod_kernel(c_ref, w_ref, b_ref, o_ref):
    c = c_ref[...]
    a = (c * jax.nn.sigmoid(c)).astype(BF16)
    o_ref[0] = jnp.dot(a, w_ref[0].astype(BF16), preferred_element_type=F32) + b_ref[0]


def _modulation(c_all, w_mod, b_mod):
    n_layers, d, n_out = w_mod.shape
    rows = c_all.shape[0]
    tn = 1024
    return pl.pallas_call(
        _mod_kernel,
        grid=(n_layers, n_out // tn),
        in_specs=[pl.BlockSpec((rows, d), lambda l, j: (0, 0)),
                  pl.BlockSpec((1, d, tn), lambda l, j: (l, 0, j)),
                  pl.BlockSpec((1, 1, tn), lambda l, j: (l, 0, j))],
        out_specs=pl.BlockSpec((1, rows, tn), lambda l, j: (l, 0, j)),
        out_shape=jax.ShapeDtypeStruct((n_layers, rows, n_out), F32),
        compiler_params=_cparams(2),
        name="modulation",
    )(c_all, w_mod, b_mod.reshape(n_layers, 1, n_out))


def _rms(t, w):
    return t * lax.rsqrt(jnp.mean(t * t, axis=-1, keepdims=True) + EPS) * w


def _rope_partner(t):
    lane = lax.broadcasted_iota(jnp.int32, t.shape, 1)
    ahead = pltpu.roll(t, LANES - 16, axis=1)
    behind = pltpu.roll(t, 16, axis=1)
    return jnp.where((lane & 16) == 0, ahead, behind)


def _inproj_kernel(x_ref, mod_ref, n1w_ref, win_ref, qanw_ref, wuq_ref, kvanw_ref, wukv_ref,
                   qnw_ref, knw_ref, rc_ref, rs_ref, wdft_ref, *rest, cols, pcols, kv_only, n_total):
    if kv_only:
        k_ref, v_ref, p0_ref, p1_ref = rest
    else:
        q_ref, k_ref, v_ref, gb_ref, g_ref, fab_ref, p0_ref, p1_ref, dft_ref = rest
    t = pl.program_id(0)

    def project(p_ref):
        state = {}

        def norm():
            shift, scale = mod_ref[0, 0:1, :], mod_ref[0, 1:2, :]
            state["hb"] = (_rms(x_ref[0], n1w_ref[...]) * (1.0 + scale) + shift).astype(BF16)

        def block(src, dst):
            def run():
                p_ref[:, dst[0]:dst[1]] = jnp.dot(state["hb"], win_ref[:, src[0]:src[1]],
                                                   preferred_element_type=F32)
            return run

        if kv_only:
            return [norm] + [block(cols[n], pcols[n]) for n in pcols]
        width = win_ref.shape[1]
        return [norm] + [block((lo, min(lo + PROJ_BLOCK, width)), (lo, min(lo + PROJ_BLOCK, width)))
                         for lo in range(0, width, PROJ_BLOCK)]

    def finish(p_ref):
        col = lambda n: p_ref[:, pcols[n][0]:pcols[n][1]]
        state = {}

        def rope(v):
            return v * rc_ref[...] + _rope_partner(v) * rs_ref[...]

        def kv_prep():
            ckv = _rms(col("ckv"), kvanw_ref[...]).astype(BF16)
            state["kv"] = jnp.dot(ckv, wukv_ref[...], preferred_element_type=F32)
            k_r = col("kr")
            state["ss_r"] = jnp.sum(k_r * k_r, axis=-1, keepdims=True)
            state["kr_rot"] = rope(k_r * knw_ref[:, LANES:])

        def k_head(h):
            def run():
                kv, v_off = state["kv"], N_HEADS * QK_NOPE
                kn = kv[:, h * QK_NOPE:(h + 1) * QK_NOPE]
                r = lax.rsqrt((jnp.sum(kn * kn, axis=-1, keepdims=True) + state["ss_r"]) * (1.0 / QK_HEAD) + EPS)
                k_ref[0, h, :, :LANES] = (kn * r * knw_ref[:, :LANES]).astype(BF16)
                k_ref[0, h, :, LANES:] = (state["kr_rot"] * r).astype(BF16)
                v_ref[0, h] = kv[:, v_off + h * V_HEAD:v_off + (h + 1) * V_HEAD].astype(BF16)
            return run

        steps = [kv_prep] + [k_head(h) for h in range(N_HEADS)]
        if kv_only:
            return steps

        def q_prep():
            cq = _rms(col("cq"), qanw_ref[...]).astype(BF16)
            state["qraw"] = jnp.dot(cq, wuq_ref[...], preferred_element_type=F32)

        def q_head(h):
            def run():
                qraw = state["qraw"]
                q0 = qraw[:, h * HEAD_SLOT:h * HEAD_SLOT + LANES]
                q1 = qraw[:, h * HEAD_SLOT + LANES:(h + 1) * HEAD_SLOT]
                ss = jnp.sum(q0 * q0, axis=-1, keepdims=True) + jnp.sum(q1 * q1, axis=-1, keepdims=True)
                r = lax.rsqrt(ss * (1.0 / QK_HEAD) + EPS) * Q_SCALE
                q_ref[0, h, :, :LANES] = (q0 * r * qnw_ref[:, :LANES]).astype(BF16)
                q_ref[0, h, :, LANES:] = rope(q1 * r * qnw_ref[:, LANES:]).astype(BF16)
            return run

        def conv_gates():
            gb_ref[0] = col("gb").astype(BF16)
            g_ref[0] = (col("gc") * col("u")).astype(BF16)

        def dft_group(g):
            def run():
                f = p_ref[:, pcols["f"][0] + g * LANES:pcols["f"][0] + (g + 1) * LANES].astype(BF16)
                res = jnp.dot(f, wdft_ref[...], preferred_element_type=F32)
                half_rows = res.shape[0] // 2
                for part in (0, 1):
                    dft_ref[2 * g + part] = res[:, part * LANES:(part + 1) * LANES]
                    for parity in (0, 1):
                        fab_ref[0, part, parity, :, g * LANES:(g + 1) * LANES] = (
                            dft_ref[2 * g + part, pl.ds(parity, half_rows, stride=2), :].astype(BF16))
            return run

        return (steps + [q_prep] + [q_head(h) for h in range(N_HEADS)] + [conv_gates] +
                [dft_group(g) for g in range(FOURIER_GROUPS)])

    _two_stage(t, n_total, project, finish, (p0_ref, p1_ref))


def _interleave(*streams):
    total = max(len(st) for st in streams)
    done = [0] * len(streams)
    for step in range(1, total + 1):
        for k, st in enumerate(streams):
            upto = -(-step * len(st) // total)
            while done[k] < upto:
                st[done[k]]()
                done[k] += 1


def _two_stage(t, n_total, first, second, bufs):
    steady = (t >= 1) & (t < n_total)
    for parity in (0, 1):
        mine = (t % 2) == parity
        new_buf, old_buf = bufs[parity], bufs[1 - parity]

        @pl.when(steady & mine)
        def _():
            _interleave(first(new_buf), second(old_buf))

        @pl.when((t == 0) & mine)
        def _():
            _interleave(first(new_buf))

        @pl.when((t == n_total) & mine)
        def _():
            _interleave(second(old_buf))


def _inproj(x, mods, lw, rope_c, rope_s, wdft, *, kv_only):
    nb, s, d = x.shape
    tm = min(s, 256)
    n_tiles = s // tm
    n_total = nb * n_tiles
    cols = lw["cols"]
    if kv_only:
        w_ckv = cols["ckv"][1] - cols["ckv"][0]
        pcols = {"ckv": (0, w_ckv), "kr": (w_ckv, w_ckv + LANES)}
    else:
        pcols = cols
    p_width = max(hi for _, hi in pcols.values())
    fw = wdft.shape[0] * FOURIER_GROUPS
    cw = cols["gb"][1] - cols["gb"][0]
    kernel = functools.partial(_inproj_kernel, cols=cols, pcols=pcols, kv_only=kv_only, n_total=n_total)
    cur = lambda t: jnp.minimum(t, n_total - 1)
    old = lambda t: jnp.maximum(t - 1, 0)
    head_spec = lambda w: pl.BlockSpec((1, N_HEADS, tm, w), lambda t: (old(t) // n_tiles, 0, old(t) % n_tiles, 0))
    tok_spec = lambda w: pl.BlockSpec((1, tm, w), lambda t: (old(t) // n_tiles, old(t) % n_tiles, 0))
    out_shape = [jax.ShapeDtypeStruct((nb, N_HEADS, s, HEAD_SLOT), BF16),
                 jax.ShapeDtypeStruct((nb, N_HEADS, s, V_HEAD), BF16)]
    out_specs = [head_spec(HEAD_SLOT), head_spec(V_HEAD)]
    if not kv_only:
        out_shape = ([jax.ShapeDtypeStruct((nb, N_HEADS, s, HEAD_SLOT), BF16)] + out_shape +
                     [jax.ShapeDtypeStruct((nb, s, cw), BF16), jax.ShapeDtypeStruct((nb, s, cw), BF16),
                      jax.ShapeDtypeStruct((nb, 2, 2, s // 2, fw), BF16)])
        out_specs = ([head_spec(HEAD_SLOT)] + out_specs +
                     [tok_spec(cw), tok_spec(cw),
                      pl.BlockSpec((1, 2, 2, tm // 2, fw),
                                   lambda t: (old(t) // n_tiles, 0, 0, old(t) % n_tiles, 0))])
    weights = [lw["n1w"], lw["w_in"], lw["qanw"], lw["w_uq"], lw["kvanw"], lw["w_ukv"], lw["qnw"], lw["knw"]]
    in_specs = ([pl.BlockSpec((1, tm, d), lambda t: (cur(t) // n_tiles, cur(t) % n_tiles, 0)),
                 pl.BlockSpec((1, 6, d), lambda t: (cur(t) // n_tiles, 0, 0))] +
                [_const_spec(w.shape) for w in weights] +
                [pl.BlockSpec((tm, LANES), lambda t: (old(t) % n_tiles, 0)),
                 pl.BlockSpec((tm, LANES), lambda t: (old(t) % n_tiles, 0)),
                 _const_spec(wdft.shape)])
    return pl.pallas_call(
        kernel, grid=(n_total + 1,), in_specs=in_specs, out_specs=out_specs, out_shape=out_shape,
        scratch_shapes=([pltpu.VMEM((tm, p_width), F32), pltpu.VMEM((tm, p_width), F32)] +
                        ([] if kv_only else [pltpu.VMEM((2 * FOURIER_GROUPS, tm, LANES), F32)])),
        compiler_params=_cparams(1), name="inproj_kv" if kv_only else "inproj",
    )(x, mods, lw["n1w"], lw["w_in"], lw["qanw"], lw["w_uq"], lw["kvanw"], lw["w_ukv"], lw["qnw"],
      lw["knw"], rope_c, rope_s, wdft)


_NT = (((1,), (1,)), ((), ()))
ATTN_HEADS_PER_STEP = 2


def _attn_kernel(q_ref, kc_ref, vc_ref, *rest, has_seq, sub):
    o_ref = rest[-1]
    key_refs = [kc_ref] + ([rest[0]] if has_seq else [])
    val_refs = [vc_ref] + ([rest[1]] if has_seq else [])

    def scores(h, r0):
        q = q_ref[0, h, r0:r0 + sub, :]
        return [lax.dot_general(q, k[0, h], _NT, preferred_element_type=F32) for k in key_refs]

    def finish(h, r0, s_parts):
        m = functools.reduce(jnp.maximum, [jnp.max(s, axis=-1, keepdims=True) for s in s_parts])
        p_parts = [jnp.exp2(s - m) for s in s_parts]
        l = sum(jnp.sum(p, axis=-1, keepdims=True) for p in p_parts)
        o = sum(jnp.dot(p.astype(BF16), v[0, h], preferred_element_type=F32) for p, v in zip(p_parts, val_refs))
        o_ref[0, r0:r0 + sub, h * V_HEAD:(h + 1) * V_HEAD] = (o / l).astype(BF16)

    items = [(h, r0) for h in range(q_ref.shape[1]) for r0 in range(0, q_ref.shape[2], sub)]
    pending = scores(*items[0])
    for n, item in enumerate(items):
        upcoming = scores(*items[n + 1]) if n + 1 < len(items) else None
        finish(*item, pending)
        pending = upcoming


def _attention(q, kc, vc, k=None, v=None):
    nb, nh, s, _ = q.shape
    n_ctx = kc.shape[2]
    has_seq = k is not None
    tq = min(s, 2048)
    hp = ATTN_HEADS_PER_STEP if has_seq else nh
    kv_spec = lambda n, w: pl.BlockSpec((1, hp, n, w), lambda b, h, i: (b, h, 0, 0))
    in_specs = [pl.BlockSpec((1, hp, tq, HEAD_SLOT), lambda b, h, i: (b, h, i, 0)),
                kv_spec(n_ctx, HEAD_SLOT), kv_spec(n_ctx, V_HEAD)]
    args = [q, kc, vc]
    if has_seq:
        in_specs += [kv_spec(s, HEAD_SLOT), kv_spec(s, V_HEAD)]
        args += [k, v]
    return pl.pallas_call(
        functools.partial(_attn_kernel, has_seq=has_seq, sub=min(tq, 256)),
        grid=(nb, nh // hp, s // tq), in_specs=in_specs,
        out_specs=pl.BlockSpec((1, tq, hp * V_HEAD), lambda b, h, i: (b, i, h)),
        out_shape=jax.ShapeDtypeStruct((nb, s, nh * V_HEAD), BF16),
        compiler_params=_cparams(3), name="attention" if has_seq else "attention_ctx",
    )(*args)


def _seqdft_kernel(ce_ref, se_ref, co_ref, so_ref, fab_ref, o_ref, *, scale):
    dot = functools.partial(jnp.dot, preferred_element_type=F32)
    even = dot(ce_ref[...], fab_ref[0, 0, 0]) + dot(se_ref[...], fab_ref[0, 1, 0])
    odd = dot(co_ref[...], fab_ref[0, 0, 1]) + dot(so_ref[...], fab_ref[0, 1, 1])
    o_ref[0, 0] = ((even + odd) * scale).astype(BF16)
    o_ref[0, 1] = ((even - odd) * scale).astype(BF16)


def _seqdft(tables, fab, group_w):
    nb, _, _, half, fw = fab.shape
    s = 2 * half
    tr = min(half, 512)
    scale = 1.0 / math.sqrt(s * group_w)
    tab = pl.BlockSpec((tr, half), lambda i, b: (i, 0))
    out = pl.pallas_call(
        functools.partial(_seqdft_kernel, scale=scale),
        grid=(half // tr, nb),
        in_specs=[tab, tab, tab, tab,
                  pl.BlockSpec((1, 2, 2, half, fw), lambda i, b: (b, 0, 0, 0, 0))],
        out_specs=pl.BlockSpec((1, 2, tr, fw), lambda i, b: (b, 0, i, 0)),
        out_shape=jax.ShapeDtypeStruct((nb, 2, half, fw), BF16),
        compiler_params=_cparams(2), name="seqdft",
    )(*tables, fab)
    return out.reshape(nb, s, fw)


HALO = 16
OUT_BLOCK = 512


def _mix_kernel(attn_ref, gb_ref, g_ref, gprev_ref, gnext_ref, fy_ref, x_ref, mod_ref, convw_ref,
                wout_ref, n2w_ref, wrcat_ref, wrhi_ref, br_ref, cntin_ref,
                x1_ref, h2_ref, route_ref, routet_ref, cnt_ref, carry_ref, hs0_ref, hs1_ref,
                *, tm, n_tiles, n_total, attn_w, conv_w):
    t = pl.program_id(0)
    i = t % n_tiles

    @pl.when(t == 0)
    def _():
        carry_ref[...] = cntin_ref[...]

    def mix_tile(hs_ref):
        state = {}

        def conv():
            g = g_ref[0].astype(F32)
            row = lax.broadcasted_iota(jnp.int32, g.shape, 0)
            prev_row = jnp.where(i > 0, gprev_ref[0, HALO - 1:HALO, :].astype(F32), 0.0)
            next_row = jnp.where(i < n_tiles - 1, gnext_ref[0, 0:1, :].astype(F32), 0.0)
            g_dn = jnp.where(row == 0, prev_row, pltpu.roll(g, 1, axis=0))
            g_up = jnp.where(row == tm - 1, next_row, pltpu.roll(g, tm - 1, axis=0))
            state["conv"] = (gb_ref[0].astype(F32) * (
                g_dn * convw_ref[0:1, :] + g * convw_ref[1:2, :] + g_up * convw_ref[2:3, :])).astype(BF16)

        def out_block(lo):
            def run():
                cs = slice(lo, lo + OUT_BLOCK)
                mix = jnp.dot(attn_ref[0], wout_ref[:attn_w, cs], preferred_element_type=F32)
                mix = mix + jnp.dot(state["conv"], wout_ref[attn_w:attn_w + conv_w, cs], preferred_element_type=F32)
                mix = mix + jnp.dot(fy_ref[0], wout_ref[attn_w + conv_w:, cs], preferred_element_type=F32)
                x1_ref[0, :, cs] = x_ref[0, :, cs] + mod_ref[0, 2:3, cs] * mix
            return run

        def norm2():
            h2 = _rms(x1_ref[0], n2w_ref[...]) * (1.0 + mod_ref[0, 4:5, :]) + mod_ref[0, 3:4, :]
            _pack_rows(h2_ref, h2)
            hs_ref[...] = h2

        return [conv] + [out_block(lo) for lo in range(0, x_ref.shape[2], OUT_BLOCK)] + [norm2]

    def route_tile(hs_ref):
        state = {}

        def logits():
            h2 = hs_ref[...]
            hi = h2.astype(BF16)
            lo = (h2 - hi.astype(F32)).astype(BF16)
            a = jnp.dot(hi, wrcat_ref[...], preferred_element_type=F32)
            state["logits"] = (a[:, :LANES] + a[:, LANES:] +
                               jnp.dot(lo, wrhi_ref[...], preferred_element_type=F32) + br_ref[...])

        def top():
            logits = state["logits"]
            lane = lax.broadcasted_iota(jnp.int32, logits.shape, 1)
            lane_f = lane.astype(F32)
            no_lane = float(LANES)
            is_group = lane < N_GROUPS
            gl = jnp.where(is_group, logits, NEG_BIG)
            gmax = jnp.max(gl, axis=-1, keepdims=True)
            gidx = jnp.min(jnp.where(gl == gmax, lane_f, no_lane), axis=-1, keepdims=True)
            g_p = 1.0 / jnp.sum(jnp.where(is_group, jnp.exp(gl - gmax), 0.0), axis=-1, keepdims=True)
            lane_group = ((lane - N_GROUPS) >> 3).astype(F32)
            in_group = (lane >= N_GROUPS) & (lane < N_GROUPS + N_EXPERTS) & (lane_group == gidx)
            el = jnp.where(in_group, logits, NEG_BIG)
            m1 = jnp.max(el, axis=-1, keepdims=True)
            i1 = jnp.min(jnp.where(el == m1, lane_f, no_lane), axis=-1, keepdims=True)
            el2 = jnp.where(lane_f == i1, NEG_BIG, el)
            m2 = jnp.max(el2, axis=-1, keepdims=True)
            i2 = jnp.min(jnp.where(el2 == m2, lane_f, no_lane), axis=-1, keepdims=True)
            e2 = jnp.exp(m2 - m1)
            state.update(lane=lane, lane_f=lane_f, i1=i1, i2=i2,
                         w1=g_p / (1.0 + e2), w2=g_p * e2 / (1.0 + e2))

        def ranks():
            lane, lane_f, i1, i2 = state["lane"], state["lane_f"], state["i1"], state["i2"]
            hit1, hit2 = lane_f == i1, lane_f == i2
            onehot = jnp.where(hit1 | hit2, 1.0, 0.0)
            ti = lax.broadcasted_iota(jnp.int32, (tm, tm), 0)
            tj = lax.broadcasted_iota(jnp.int32, (tm, tm), 1)
            before = jnp.where(tj < ti, 1.0, 0.0).astype(BF16)
            seen = jnp.dot(before, onehot.astype(BF16), preferred_element_type=F32) + carry_ref[...]
            r1 = jnp.sum(jnp.where(hit1, seen, 0.0), axis=-1, keepdims=True)
            r2 = jnp.sum(jnp.where(hit2, seen, 0.0), axis=-1, keepdims=True)
            carry_ref[...] = carry_ref[...] + jnp.sum(onehot, axis=0, keepdims=True)
            cnt_ref[...] = carry_ref[...]

            route = jnp.where(lane == 0, i1 - N_GROUPS, 0.0)
            route = jnp.where(lane == 1, i2 - N_GROUPS, route)
            route = jnp.where(lane == 2, r1, route)
            route = jnp.where(lane == 3, r2, route)
            route = jnp.where(lane == 4, state["w1"], route)
            route = jnp.where(lane == 5, state["w2"], route)
            route_ref[...] = route
            routet_ref[...] = route.T[:ROUTE_ROWS, :]

        return [logits, top, ranks]

    _two_stage(t, n_total, mix_tile, route_tile, (hs0_ref, hs1_ref))


def _mix(attn, gb, g, fy, x, mods, lw, cnt_in):
    nb, s, d = x.shape
    tm = min(s, 256)
    n_tiles = s // tm
    n_total = nb * n_tiles
    attn_w, conv_w = attn.shape[-1], gb.shape[-1]
    cur = lambda t: jnp.minimum(t, n_total - 1)
    old = lambda t: jnp.maximum(t - 1, 0)
    tok = lambda w: pl.BlockSpec((1, tm, w), lambda t: (cur(t) // n_tiles, cur(t) % n_tiles, 0))
    hb = tm // HALO
    in_specs = [tok(attn_w), tok(conv_w), tok(conv_w),
                pl.BlockSpec((1, HALO, conv_w),
                             lambda t: (cur(t) // n_tiles, jnp.maximum((cur(t) % n_tiles) * hb - 1, 0), 0)),
                pl.BlockSpec((1, HALO, conv_w),
                             lambda t: (cur(t) // n_tiles, jnp.minimum((cur(t) % n_tiles + 1) * hb, s // HALO - 1), 0)),
                tok(fy.shape[-1]), tok(d),
                pl.BlockSpec((1, 6, d), lambda t: (cur(t) // n_tiles, 0, 0))]
    weights = [lw["conv_w"], lw["w_out"], lw["n2w"], lw["wr_cat"], lw["wr_hi"], lw["b_r"], cnt_in]
    in_specs += [_const_spec(w.shape) for w in weights]
    n_tok = nb * s
    out_shape = [jax.ShapeDtypeStruct((nb, s, d), F32),
                 jax.ShapeDtypeStruct((n_tok * WORD_ROWS, LANES), U32),
                 jax.ShapeDtypeStruct((n_tok, LANES), F32),
                 jax.ShapeDtypeStruct((ROUTE_ROWS, n_tok), F32),
                 jax.ShapeDtypeStruct((1, LANES), F32)]
    out_specs = [tok(d),
                 pl.BlockSpec((tm * WORD_ROWS, LANES), lambda t: (cur(t), 0)),
                 pl.BlockSpec((tm, LANES), lambda t: (old(t), 0)),
                 pl.BlockSpec((ROUTE_ROWS, tm), lambda t: (0, old(t))),
                 pl.BlockSpec((1, LANES), lambda t: (0, 0))]
    return pl.pallas_call(
        functools.partial(_mix_kernel, tm=tm, n_tiles=n_tiles, n_total=n_total, attn_w=attn_w, conv_w=conv_w),
        grid=(n_total + 1,), in_specs=in_specs, out_specs=out_specs, out_shape=out_shape,
        scratch_shapes=[pltpu.VMEM((1, LANES), F32), pltpu.VMEM((tm, d), F32), pltpu.VMEM((tm, d), F32)],
        compiler_params=_cparams(1), name="mix_router",
    )(attn, gb, g, g, g, fy, x, mods, *weights)


def _row(ref, idx):
    return ref.at[pl.ds(pl.multiple_of(idx * WORD_ROWS, WORD_ROWS), WORD_ROWS), :]


DMA_UNROLL = 8
ZERO_ROWS = 128
WEIGHT_DMA_PRIORITY = 1


def _row_dma_loop(n_rows, copies, method, alternate=False):
    def body(jo, carry):
        n = 0
        for u in range(DMA_UNROLL):
            for c in copies(jo * DMA_UNROLL + u):
                if method == "start":
                    c.start(priority=n % 2 if alternate else 0)
                else:
                    c.wait()
                n += 1
        return carry
    lax.fori_loop(0, n_rows // DMA_UNROLL, body, 0)


def _tile_slots(slots_kt, tm):
    n_tok = slots_kt.shape[1]
    return slots_kt.reshape(TOP_K, n_tok // tm, tm).transpose(1, 0, 2).reshape(n_tok // tm, 1, TOP_K * tm)


def _dispatch_kernel(lo_ref, n_ref, slot_ref, *refs, tm, n_main_tiles, has_ctx):
    if has_ctx:
        h2_ref, h2c_ref, xs_ref, zero_ref, sem, zsem = refs
    else:
        h2_ref, xs_ref, zero_ref, sem, zsem = refs
    i = pl.program_id(0)

    @pl.when(i == 0)
    def _():
        zero_ref[...] = jnp.zeros(zero_ref.shape, U32)

        def zero_copy(first, rows):
            return pltpu.make_async_copy(
                zero_ref.at[pl.ds(0, rows * WORD_ROWS), :],
                xs_ref.at[pl.ds(pl.multiple_of(first * WORD_ROWS, WORD_ROWS), rows * WORD_ROWS), :], zsem)

        def fill(e, carry):
            lo, n = lo_ref[e], n_ref[e]
            n_full = n // ZERO_ROWS

            def run(method):
                def full(c, cc):
                    getattr(zero_copy(lo + c * ZERO_ROWS, ZERO_ROWS), method)()
                    return cc
                lax.fori_loop(0, n_full, full, 0)
                first = lo + n_full * ZERO_ROWS
                p = ZERO_ROWS // 2
                while p >= 1:
                    @pl.when((n & p) != 0)
                    def _(first=first, p=p):
                        getattr(zero_copy(first, p), method)()
                    first = first + (n & p)
                    p //= 2
            run("start")
            run("wait")
            return carry

        lax.fori_loop(0, lo_ref.shape[0], fill, 0)

    def scatter(src_ref):
        copies = lambda j: [
            pltpu.make_async_copy(_row(src_ref, j), _row(xs_ref, slot_ref[0, 0, k * tm + j]), sem)
            for k in range(TOP_K)]
        _row_dma_loop(tm, copies, "start", alternate=True)
        _row_dma_loop(tm, copies, "wait")

    if has_ctx:
        @pl.when(i < n_main_tiles)
        def _():
            scatter(h2_ref)

        @pl.when(i >= n_main_tiles)
        def _():
            scatter(h2c_ref)
    else:
        scatter(h2_ref)


def _dispatch(slots_kt, fill_lo, fill_n, h2, h2c, n_slots):
    tm = MOE_BLOCK
    n_main_tiles = h2.shape[0] // (tm * WORD_ROWS)
    has_ctx = h2c is not None
    slots3 = _tile_slots(slots_kt, tm)
    n_tiles = slots3.shape[0]
    blk = (tm * WORD_ROWS, LANES)
    in_specs = [pl.BlockSpec((1, 1, tm * TOP_K), lambda i, lo, n: (i, 0, 0), memory_space=pltpu.SMEM),
                pl.BlockSpec(blk, lambda i, lo, n: (jnp.minimum(i, n_main_tiles - 1), 0))]
    args = [slots3, h2]
    if has_ctx:
        in_specs.append(pl.BlockSpec(blk, lambda i, lo, n: (jnp.maximum(i - n_main_tiles, 0), 0)))
        args.append(h2c)
    grid_spec = pltpu.PrefetchScalarGridSpec(
        num_scalar_prefetch=2, grid=(n_tiles,), in_specs=in_specs,
        out_specs=pl.BlockSpec(memory_space=pl.ANY),
        scratch_shapes=[pltpu.VMEM((ZERO_ROWS * WORD_ROWS, LANES), U32),
                        pltpu.SemaphoreType.DMA(()), pltpu.SemaphoreType.DMA(())])
    return pl.pallas_call(
        functools.partial(_dispatch_kernel, tm=tm, n_main_tiles=n_main_tiles, has_ctx=has_ctx),
        grid_spec=grid_spec,
        out_shape=jax.ShapeDtypeStruct((n_slots * WORD_ROWS, LANES), U32),
        compiler_params=_cparams(1), name="dispatch",
    )(fill_lo, fill_n, *args)


def _expert_kernel(bexp_ref, nused_ref, nxt_ref, par_ref, xs_ref, wg_hbm, wu_hbm, wd_hbm, y_ref,
                   xb0_ref, xb1_ref, yb0_ref, yb1_ref, wgu_ref, wdb_ref, wgf_ref, wuf_ref, wdf_ref, wsem,
                   *, layer, d_expert, n_blocks):
    s = pl.program_id(0)
    n_used = nused_ref[0]
    half = xb0_ref.shape[1] // 2

    def unpack(xb_ref):
        def rows(r):
            def run():
                lo, hi = _unpack_pair(xs_ref[pl.ds(r, MOE_BLOCK, stride=WORD_ROWS), :])
                xb_ref[:, r * LANES:(r + 1) * LANES] = lo.astype(BF16)
                xb_ref[:, half + r * LANES:half + (r + 1) * LANES] = hi.astype(BF16)
            return run
        return [rows(r) for r in range(WORD_ROWS)]

    def mlp(xb_ref, yb_ref):
        state = {}

        def gate_up():
            gu = jnp.dot(xb_ref[...], wgu_ref[...], preferred_element_type=F32)
            gate, up = gu[:, :d_expert], gu[:, d_expert:]
            state["hid"] = (gate * jax.nn.sigmoid(gate) * up).astype(BF16)

        def down(lo):
            def run():
                yb_ref[:, lo:lo + half] = jnp.dot(state["hid"], wdb_ref[:, lo:lo + half], preferred_element_type=F32)
            return run

        return [gate_up, down(0), down(half)]

    def store(yb_ref):
        def rows(r):
            def run():
                y_ref[pl.ds(r, MOE_BLOCK, stride=WORD_ROWS), :] = _pack_pair(
                    yb_ref[:, r * LANES:(r + 1) * LANES], yb_ref[:, half + r * LANES:half + (r + 1) * LANES])
            return run
        return [rows(r) for r in range(WORD_ROWS)]

    def weight_copies(e, buf):
        return [pltpu.make_async_copy(src.at[layer, e], dst.at[buf], wsem.at[buf])
                for src, dst in ((wg_hbm, wgf_ref), (wu_hbm, wuf_ref), (wd_hbm, wdf_ref))]

    run_unpack = s < n_used
    run_mlp = (s >= 1) & (s <= n_used)
    run_store = (s >= 2) & (s <= n_used + 1)
    steady = (s >= 2) & (s < n_used)

    mlp_block = jnp.clip(s - 1, 0, n_blocks - 1)
    new_expert = (s == 1) | (bexp_ref[mlp_block] != bexp_ref[jnp.maximum(mlp_block - 1, 0)])

    @pl.when(s == 0)
    def _():
        for c in weight_copies(bexp_ref[0], 0):
            c.start(priority=WEIGHT_DMA_PRIORITY)

    @pl.when(run_mlp & new_expert)
    def _():
        e = bexp_ref[mlp_block]
        buf = par_ref[e]
        for c in weight_copies(e, buf):
            c.wait()
        nxt = nxt_ref[e]

        @pl.when(nxt < N_EXPERTS)
        def _():
            for c in weight_copies(nxt, 1 - buf):
                c.start(priority=WEIGHT_DMA_PRIORITY)

        wgu_ref[:, :d_expert] = wgf_ref[buf].astype(BF16)
        wgu_ref[:, d_expert:] = wuf_ref[buf].astype(BF16)
        wdb_ref[...] = wdf_ref[buf].astype(BF16)

    for parity, (xb_new, xb_cur, yb_cur, yb_old) in enumerate(
            [(xb0_ref, xb1_ref, yb1_ref, yb0_ref), (xb1_ref, xb0_ref, yb0_ref, yb1_ref)]):
        mine = (s % 2) == parity

        @pl.when(steady & mine)
        def _():
            _interleave(mlp(xb_cur, yb_cur), unpack(xb_new), store(yb_old))

        @pl.when(jnp.logical_not(steady) & mine)
        def _():
            @pl.when(run_unpack)
            def _():
                _interleave(unpack(xb_new))

            @pl.when(run_mlp)
            def _():
                _interleave(mlp(xb_cur, yb_cur))

            @pl.when(run_store)
            def _():
                _interleave(store(yb_old))

    @pl.when(s >= n_used + 2)
    def _():
        y_ref[...] = jnp.zeros(y_ref.shape, U32)


def _experts(layer, block_exp, n_used, next_exp, run_par, xs, w_gate, w_up, w_down):
    _, _, d, d_expert = w_gate.shape
    n_blocks = block_exp.shape[0]
    blk = (MOE_BLOCK * WORD_ROWS, LANES)
    hbm = pl.BlockSpec(memory_space=pl.ANY)
    grid_spec = pltpu.PrefetchScalarGridSpec(
        num_scalar_prefetch=4, grid=(n_blocks + 2,),
        in_specs=[pl.BlockSpec(blk, lambda s, be, nu, nx, pr: (jnp.minimum(s, nu[0] - 1), 0)), hbm, hbm, hbm],
        out_specs=pl.BlockSpec(blk, lambda s, be, nu, nx, pr: (jnp.maximum(s - 2, 0), 0)),
        scratch_shapes=[pltpu.VMEM((MOE_BLOCK, d), BF16), pltpu.VMEM((MOE_BLOCK, d), BF16),
                        pltpu.VMEM((MOE_BLOCK, d), F32), pltpu.VMEM((MOE_BLOCK, d), F32),
                        pltpu.VMEM((d, 2 * d_expert), BF16),
                        pltpu.VMEM((d_expert, d), BF16),
                        pltpu.VMEM((2, d, d_expert), F32), pltpu.VMEM((2, d, d_expert), F32),
                        pltpu.VMEM((2, d_expert, d), F32),
                        pltpu.SemaphoreType.DMA((2,))])
    return pl.pallas_call(
        functools.partial(_expert_kernel, layer=layer, d_expert=d_expert, n_blocks=n_blocks),
        grid_spec=grid_spec,
        out_shape=jax.ShapeDtypeStruct(xs.shape, U32),
        compiler_params=_cparams(1), name="experts",
    )(block_exp, n_used, next_exp, run_par, xs, w_gate, w_up, w_down)


def _combine_kernel(slot_ref, slotn_ref, x1_ref, mod_ref, route_ref, y_ref, o_ref, ybuf_ref, sem, *, tm, n_total):
    t = pl.program_id(0)
    half_rows = TOP_K * tm
    cur = t % 2

    def gathers(slots, buf):
        return lambda j: [
            pltpu.make_async_copy(_row(y_ref, slots[0, 0, k * tm + j]), _row(ybuf_ref, buf * half_rows + k * tm + j),
                                  sem.at[buf])
            for k in range(TOP_K)]

    @pl.when(t == 0)
    def _():
        _row_dma_loop(tm, gathers(slot_ref, 0), "start", alternate=True)

    @pl.when(t + 1 < n_total)
    def _():
        _row_dma_loop(tm, gathers(slotn_ref, 1 - cur), "start", alternate=True)

    _row_dma_loop(tm, gathers(slot_ref, cur), "wait")

    w1, w2 = route_ref[:, 4:5], route_ref[:, 5:6]
    base = cur * half_rows * WORD_ROWS
    half = x1_ref.shape[2] // 2
    for r in range(WORD_ROWS):
        a_lo, a_hi = _unpack_pair(ybuf_ref[pl.ds(base + r, tm, stride=WORD_ROWS), :])
        b_lo, b_hi = _unpack_pair(ybuf_ref[pl.ds(base + tm * WORD_ROWS + r, tm, stride=WORD_ROWS), :])
        for off, ya, yb in ((0, a_lo, b_lo), (half, a_hi, b_hi)):
            lanes = slice(off + r * LANES, off + (r + 1) * LANES)
            o_ref[0, :, lanes] = x1_ref[0, :, lanes] + mod_ref[0, 5:6, lanes] * (ya * w1 + yb * w2)


def _combine(slots_kt, x1, mods, route, y):
    nb, s, d = x1.shape
    tm = min(s, MOE_BLOCK)
    n_tiles = s // tm
    n_total = nb * n_tiles
    slots3 = _tile_slots(slots_kt, tm)
    slot_spec = lambda nxt: pl.BlockSpec((1, 1, tm * TOP_K), lambda t: (jnp.minimum(t + nxt, n_total - 1), 0, 0),
                                         memory_space=pltpu.SMEM)
    return pl.pallas_call(
        functools.partial(_combine_kernel, tm=tm, n_total=n_total),
        grid=(n_total,),
        in_specs=[slot_spec(0), slot_spec(1),
                  pl.BlockSpec((1, tm, d), lambda t: (t // n_tiles, t % n_tiles, 0)),
                  pl.BlockSpec((1, 6, d), lambda t: (t // n_tiles, 0, 0)),
                  pl.BlockSpec((tm, LANES), lambda t: (t, 0)),
                  pl.BlockSpec(memory_space=pl.ANY)],
        out_specs=pl.BlockSpec((1, tm, d), lambda t: (t // n_tiles, t % n_tiles, 0)),
        out_shape=jax.ShapeDtypeStruct((nb, s, d), F32),
        scratch_shapes=[pltpu.VMEM((2 * TOP_K * tm * WORD_ROWS, LANES), U32), pltpu.SemaphoreType.DMA((2,))],
        compiler_params=_cparams(1), name="combine",
    )(slots3, slots3, x1, mods, route, y)


def _layer_weights(l, d, norm1_w, norm2_w, w_in, q_a_norm_w, w_uq, kv_a_norm_w, w_ukv, q_norm_w, k_norm_w,
                   conv_w, w_out, w_rg, b_rg, w_re, b_re):
    q_lora, kv_lora = q_a_norm_w.shape[1], kv_a_norm_w.shape[1]
    conv_cols = conv_w.shape[2]
    in_cols = w_in.shape[2]
    four_cols = in_cols - (q_lora + kv_lora + QK_ROPE + 3 * conv_cols)
    o_ckv = q_lora
    o_kr = o_ckv + kv_lora
    o_gb = o_kr + QK_ROPE
    o_gc, o_u, o_f = o_gb + conv_cols, o_gb + 2 * conv_cols, o_gb + 3 * conv_cols
    wi = w_in[l]
    w_in_p = jnp.concatenate([wi[:, :o_kr], wi[:, o_gb:], wi[:, o_kr:o_gb],
                              jnp.zeros((d, LANES - QK_ROPE), F32)], axis=1).astype(BF16)
    names, widths = ["cq", "ckv", "gb", "gc", "u", "f", "kr"], [q_lora, kv_lora, conv_cols, conv_cols, conv_cols, four_cols, LANES]
    cols, o = {}, 0
    for n, w in zip(names, widths):
        cols[n] = (o, o + w)
        o += w
    pad_head = lambda w: jnp.pad(w, [(0, 0)] * (w.ndim - 1) + [(0, HEAD_SLOT - QK_HEAD)])
    w_uq_p = pad_head(w_uq[l].reshape(q_lora, N_HEADS, QK_HEAD)).reshape(q_lora, N_HEADS * HEAD_SLOT).astype(BF16)
    w_ukv_p = (w_ukv[l].reshape(kv_lora, N_HEADS, 2, QK_NOPE).transpose(0, 2, 1, 3)
               .reshape(kv_lora, 2 * N_HEADS * QK_NOPE).astype(BF16))
    w_r = jnp.concatenate([w_rg[l], w_re[l], jnp.zeros((d, LANES - N_GROUPS - N_EXPERTS), F32)], axis=1)
    wr_hi = w_r.astype(BF16)
    wr_lo = (w_r - wr_hi.astype(F32)).astype(BF16)
    b_r = jnp.concatenate([b_rg[l], b_re[l], jnp.zeros((LANES - N_GROUPS - N_EXPERTS,), F32)]).reshape(1, LANES)
    return dict(cols=cols, n1w=norm1_w[l].reshape(1, d), n2w=norm2_w[l].reshape(1, d), w_in=w_in_p,
                qanw=q_a_norm_w[l].reshape(1, q_lora), w_uq=w_uq_p, kvanw=kv_a_norm_w[l].reshape(1, kv_lora),
                w_ukv=w_ukv_p, qnw=pad_head(q_norm_w[l]).reshape(1, HEAD_SLOT),
                knw=pad_head(k_norm_w[l]).reshape(1, HEAD_SLOT), conv_w=conv_w[l], w_out=w_out[l].astype(BF16),
                wr_cat=jnp.concatenate([wr_hi, wr_lo], axis=1), wr_hi=wr_hi, b_r=b_r)


def _rope_tables(n_tok):
    freqs = QK_ROPE // 4
    pos = jnp.arange(n_tok)
    row = (pos // GRID_W).astype(F32)
    colp = (pos % GRID_W).astype(F32)
    inv = ROPE_THETA ** (-jnp.arange(freqs, dtype=F32) / freqs)
    ar, ac = row[:, None] * inv, colp[:, None] * inv
    zeros = jnp.zeros((n_tok, LANES - QK_ROPE), F32)
    rope_c = jnp.concatenate([jnp.cos(ar), jnp.cos(ar), jnp.cos(ac), jnp.cos(ac), zeros], axis=1)
    rope_s = jnp.concatenate([-jnp.sin(ar), jnp.sin(ar), -jnp.sin(ac), jnp.sin(ac), zeros], axis=1)
    return rope_c, rope_s


def _identity_rope(n_tok):
    ones = jnp.concatenate([jnp.ones((n_tok, QK_ROPE), F32), jnp.zeros((n_tok, LANES - QK_ROPE), F32)], axis=1)
    return ones, jnp.zeros((n_tok, LANES), F32)


def _dft_tables(n):
    idx = jnp.arange(n, dtype=jnp.int32)
    ang = ((idx[:, None] * idx[None, :]) % n).astype(F32) * (2.0 * math.pi / n)
    return jnp.cos(ang), jnp.sin(ang)


def _split_dft_tables(n):
    half = n // 2
    j = jnp.arange(half, dtype=jnp.int32)[:, None]
    m = jnp.arange(half, dtype=jnp.int32)[None, :]
    wrap = (lambda v: v & (n - 1)) if n & (n - 1) == 0 else (lambda v: v % n)
    even = wrap(j * 2 * m).astype(F32) * (2.0 * math.pi / n)
    odd = wrap(j * (2 * m + 1)).astype(F32) * (2.0 * math.pi / n)
    return tuple(t.astype(BF16) for t in (jnp.cos(even), jnp.sin(even), jnp.cos(odd), jnp.sin(odd)))


def _slot_plan(route_t, counts):
    n_tok = route_t.shape[1]
    expert = route_t[0:TOP_K].astype(jnp.int32).reshape(-1)
    rank = route_t[TOP_K:2 * TOP_K].astype(jnp.int32).reshape(-1)
    padded = (counts + MOE_BLOCK - 1) // MOE_BLOCK * MOE_BLOCK
    pad_ends = jnp.cumsum(padded)
    pad_starts = pad_ends - padded
    base = functools.reduce(lambda acc, e: jnp.where(expert == e, pad_starts[e], acc), range(N_EXPERTS),
                            jnp.zeros_like(expert))
    slots = (base + rank).reshape(TOP_K, n_tok)
    n_blocks = -(-(n_tok * TOP_K) // MOE_BLOCK) + N_EXPERTS
    n_slots = n_blocks * MOE_BLOCK
    block_start = jnp.arange(n_blocks, dtype=jnp.int32) * MOE_BLOCK
    block_exp = jnp.sum((pad_ends[None, :] <= block_start[:, None]).astype(jnp.int32), axis=1)
    block_exp = jnp.minimum(block_exp, N_EXPERTS - 1)
    n_used = (pad_ends[-1] // MOE_BLOCK).astype(jnp.int32)
    last_exp = block_exp[jnp.maximum(n_used - 1, 0)]
    block_exp = jnp.where(jnp.arange(n_blocks) < n_used, block_exp, last_exp)
    fill_lo = jnp.concatenate([pad_starts + counts, pad_ends[-1:]]).astype(jnp.int32)
    fill_n = jnp.concatenate([padded - counts, n_slots - pad_ends[-1:]]).astype(jnp.int32)
    ids = jnp.arange(N_EXPERTS, dtype=jnp.int32)
    live_id = jnp.where(counts > 0, ids, N_EXPERTS)
    live_from = lax.cummin(live_id[::-1])[::-1]
    next_exp = jnp.concatenate([live_from[1:], jnp.full((1,), N_EXPERTS, jnp.int32)])
    run_par = (jnp.cumsum(counts > 0) - (counts > 0)).astype(jnp.int32) & 1
    return (slots.astype(jnp.int32), block_exp, n_used.reshape(1), next_exp.astype(jnp.int32), run_par,
            fill_lo, fill_n, n_slots)


def kernel(x, c, ctx, c_ctx, w_mod, b_mod, norm1_w, norm2_w, w_in, q_a_norm_w, w_uq, kv_a_norm_w, w_ukv, q_norm_w, k_norm_w, conv_w, w_out, w_router_group, b_router_group, w_router_expert, b_router_expert, w_gate, w_up, w_down):
    nb, s, d = x.shape
    n_ctx = ctx.shape[1]
    depth = w_mod.shape[0]
    group_w = conv_w.shape[2] // FOURIER_GROUPS

    rows = -(-(nb + 1) // 8) * 8
    c_all = jnp.concatenate([c, c_ctx[None, :], jnp.zeros((rows - nb - 1, d), F32)], axis=0)
    mod_all = _modulation(c_all, w_mod, b_mod)

    rope_c, rope_s = _rope_tables(s)
    id_c, id_s = _identity_rope(n_ctx)
    cc, sc = _dft_tables(group_w)
    wdft = jnp.concatenate([cc, -sc], axis=1).astype(BF16)
    seq_tables = _split_dft_tables(s)
    ctx_tables = _split_dft_tables(n_ctx)

    xc = ctx
    for l in range(depth):
        last = l == depth - 1
        lw = _layer_weights(l, d, norm1_w, norm2_w, w_in, q_a_norm_w, w_uq, kv_a_norm_w, w_ukv, q_norm_w,
                            k_norm_w, conv_w, w_out, w_router_group, b_router_group, w_router_expert,
                            b_router_expert)
        mods = mod_all[l, :nb].reshape(nb, 6, d)
        mods_c = jnp.broadcast_to(mod_all[l, nb].reshape(1, 6, d), (nb, 6, d))

        q, k, v, gb, g, fab = _inproj(x, mods, lw, rope_c, rope_s, wdft, kv_only=False)
        if last:
            kc, vc = _inproj(xc, mods_c, lw, id_c, id_s, wdft, kv_only=True)
        else:
            qc, kc, vc, gbc, gc, fabc = _inproj(xc, mods_c, lw, id_c, id_s, wdft, kv_only=False)
        attn = _attention(q, kc, vc, k, v)
        fy = _seqdft(seq_tables, fab, group_w)
        zero_cnt = jnp.zeros((1, LANES), F32)
        x1, h2, route, route_t, cnt = _mix(attn, gb, g, fy, x, mods, lw, zero_cnt)
        if not last:
            attn_c = _attention(qc, kc, vc)
            fyc = _seqdft(ctx_tables, fabc, group_w)
            x1c, h2c, route_c, route_tc, cnt = _mix(attn_c, gbc, gc, fyc, xc, mods_c, lw, cnt)
            route_t = jnp.concatenate([route_t, route_tc], axis=1)
        else:
            h2c = None

        counts = cnt[0, N_GROUPS:N_GROUPS + N_EXPERTS].astype(jnp.int32)
        slots, block_exp, n_used, next_exp, run_par, fill_lo, fill_n, n_slots = _slot_plan(route_t, counts)
        xs = _dispatch(slots, fill_lo, fill_n, h2, h2c, n_slots)
        y = _experts(l, block_exp, n_used, next_exp, run_par, xs, w_gate, w_up, w_down)
        n_main = nb * s
        x = _combine(slots[:, :n_main], x1, mods, route, y)
        if not last:
            xc = _combine(slots[:, n_main:], x1c, mods_c, route_c, y)
    return x
```
